```python
import math
import jax
import jax.numpy as jnp
from jax import lax
import numpy as np

D_MODEL = 1024
BATCH = 1
SEQ = 16384
DEPTH = 1
DEC_BATCH = 128
DEC_SEQ = 1
PAST_LEN = 8192
PAGE_SIZE = 128

N_META = 16
N_HEADS = 16
HEAD_DIM = 64
ATT_DIM = N_HEADS * HEAD_DIM
IDX_HEADS = 8
IDX_DIM = 64
TOPK_MAX = 256
N_BUCKETS = 32
MAX_DISTANCE = 128
Q_BLOCK = 128
SSM_INNER = 2 * D_MODEL
SSM_HEAD_DIM = 64
SSM_HEADS = SSM_INNER // SSM_HEAD_DIM
SSM_GROUPS = 4
SSM_STATE = 128
CONV_WIDTH = 4
CONV_DIM = SSM_INNER + 2 * SSM_GROUPS * SSM_STATE
CHUNK = 128
FFN_HIDDEN = -(-8 * D_MODEL // (3 * 256)) * 256
IN_DIM = 3 * ATT_DIM + IDX_HEADS * IDX_DIM + IDX_DIM + IDX_HEADS + SSM_INNER + CONV_DIM + SSM_HEADS + 2 * D_MODEL
EPS = 1e-6
DT_MIN = 1e-3
DT_MAX = 1e-1

kernel_name = 'hybrid_dsa_ssd_gated_decoder_step'


def rmsnorm(x, g):
    xf = x.astype(jnp.float32)
    y = xf * lax.rsqrt(jnp.mean(xf * xf, axis=-1, keepdims=True) + EPS)
    return (y * g.astype(jnp.float32)).astype(x.dtype)


def t5_bucket(dist):
    max_exact = N_BUCKETS // 2
    d = jnp.maximum(dist, 0)
    df = jnp.maximum(d, 1).astype(jnp.float32)
    large = max_exact + (jnp.log(df / max_exact) / math.log(MAX_DISTANCE / max_exact)
                         * (N_BUCKETS - max_exact)).astype(jnp.int32)
    return jnp.where(d < max_exact, d, jnp.minimum(large, N_BUCKETS - 1))


def in_projection(h, w_in):
    p = jnp.einsum('btd,de->bte', h, w_in)
    sizes = (ATT_DIM, ATT_DIM, ATT_DIM, IDX_HEADS * IDX_DIM, IDX_DIM, IDX_HEADS,
             SSM_INNER, CONV_DIM, SSM_HEADS, D_MODEL, D_MODEL)
    points = [int(v) for v in np.cumsum(sizes)[:-1]]
    q, k, v, qi, ki, wi, z, xbc, dtr, ga, gb = jnp.split(p, points, axis=-1)
    b, t = h.shape[:2]
    return (q.reshape(b, t, N_HEADS, HEAD_DIM), k.reshape(b, t, N_HEADS, HEAD_DIM),
            v.reshape(b, t, N_HEADS, HEAD_DIM),
            qi.reshape(b, t, IDX_HEADS, IDX_DIM) * IDX_DIM ** -0.5, ki, wi * IDX_HEADS ** -0.5,
            z, xbc, dtr, ga, gb)


def indexer_scores(q_idx, w_idx, k_idx):
    s = jnp.einsum('bqhd,bsd->bqhs', q_idx, k_idx).astype(jnp.float32)
    return jnp.einsum('bqhs,bqh->bqs', jax.nn.relu(s), w_idx.astype(jnp.float32))


def sparse_attend(q, k_sel, v_sel, dist, valid, rel_bias):
    logits = jnp.einsum('bqhd,bqkhd->bqhk', q, k_sel).astype(jnp.float32) * HEAD_DIM ** -0.5
    bias = jnp.moveaxis(rel_bias[t5_bucket(dist)], -1, -2).astype(jnp.float32)
    logits = jnp.where(valid[:, :, None, :], logits + bias, -jnp.inf)
    p = jax.nn.softmax(logits, axis=-1)
    return jnp.einsum('bqhk,bqkhd->bqhd', p.astype(v_sel.dtype), v_sel)


def prompt_sparse_attention(q, k, v, q_idx, w_idx, k_idx, rel_bias):
    b, t = q.shape[:2]
    topk = min(TOPK_MAX, t // 4)
    n_blk = -(-t // Q_BLOCK)
    pad = n_blk * Q_BLOCK - t

    def blocks(a):
        a = jnp.pad(a, [(0, 0), (0, pad)] + [(0, 0)] * (a.ndim - 2))
        return jnp.swapaxes(a.reshape((b, n_blk, Q_BLOCK) + a.shape[2:]), 0, 1)

    key_pos = jnp.arange(t, dtype=jnp.int32)
    take = jax.vmap(lambda rows, idx: rows[idx])

    def one_block(args):
        blk, qb, qib, wb = args
        q_pos = blk * Q_BLOCK + jnp.arange(Q_BLOCK, dtype=jnp.int32)
        causal = key_pos[None, None, :] <= q_pos[None, :, None]
        score = jnp.where(causal, indexer_scores(qib, wb, k_idx), -jnp.inf)
        _, sel = lax.top_k(score, topk)
        valid = sel <= q_pos[None, :, None]
        return sparse_attend(qb, take(k, sel), take(v, sel), q_pos[None, :, None] - sel, valid, rel_bias)

    out = lax.map(one_block, (jnp.arange(n_blk, dtype=jnp.int32), blocks(q), blocks(q_idx), blocks(w_idx)))
    return jnp.swapaxes(out, 0, 1).reshape(b, n_blk * Q_BLOCK, ATT_DIM)[:, :t]


def sample_sparse_attention(q, k_new, v_new, q_idx, w_idx, kidx_new, cache_k, cache_v, cache_idx_k,
                            page_table, rel_bias):
    b, s = q.shape[:2]
    past = page_table.shape[1] * PAGE_SIZE
    total = past + s
    topk = min(TOPK_MAX, total // 4)
    kidx_past = cache_idx_k[page_table].reshape(b, past, IDX_DIM)
    kidx_all = jnp.concatenate([kidx_past, kidx_new], axis=1)
    q_pos = past + jnp.arange(s, dtype=jnp.int32)
    causal = jnp.arange(total, dtype=jnp.int32)[None, None, :] <= q_pos[None, :, None]
    score = jnp.where(causal, indexer_scores(q_idx, w_idx, kidx_all), -jnp.inf)
    _, sel = lax.top_k(score, topk)
    valid = sel <= q_pos[None, :, None]
    from_past = (sel < past)[..., None, None]
    sel_past = jnp.minimum(sel, past - 1)
    phys = jax.vmap(lambda pt, idx: pt[idx])(page_table, sel_past // PAGE_SIZE)
    slot = sel_past % PAGE_SIZE
    sel_new = jnp.clip(sel - past, 0, s - 1)
    take = jax.vmap(lambda rows, idx: rows[idx])
    k_sel = jnp.where(from_past, cache_k[phys, slot], take(k_new, sel_new))
    v_sel = jnp.where(from_past, cache_v[phys, slot], take(v_new, sel_new))
    out = sparse_attend(q, k_sel, v_sel, q_pos[None, :, None] - sel, valid, rel_bias)
    return out.reshape(b, s, ATT_DIM)


def causal_conv(x_hist, w, bias):
    t = x_hist.shape[1] - (CONV_WIDTH - 1)
    return sum(x_hist[:, j:j + t] * w[j] for j in range(CONV_WIDTH)) + bias


def ssm_conv_split(x_hist, conv_w, conv_b):
    xbc = jax.nn.silu(causal_conv(x_hist, conv_w, conv_b))
    b, t = xbc.shape[:2]
    xs, bm, cm = jnp.split(xbc, [SSM_INNER, SSM_INNER + SSM_GROUPS * SSM_STATE], axis=-1)
    return (xs.reshape(b, t, SSM_HEADS, SSM_HEAD_DIM), bm.reshape(b, t, SSM_GROUPS, SSM_STATE),
            cm.reshape(b, t, SSM_GROUPS, SSM_STATE))


def ssd_chunked(x, dt, a, bm, cm):
    b, t = x.shape[:2]
    e = SSM_HEADS // SSM_GROUPS
    pad = (-t) % CHUNK
    nc = (t + pad) // CHUNK

    def chunks(arr, tail):
        arr = jnp.pad(arr.astype(jnp.float32), [(0, 0), (pad, 0)] + [(0, 0)] * (arr.ndim - 2))
        return jnp.swapaxes(arr.reshape((b, nc, CHUNK) + tail), 0, 1)

    xdt = chunks(x.astype(jnp.float32) * dt[..., None], (SSM_GROUPS, e, SSM_HEAD_DIM))
    la = chunks(dt * a, (SSM_GROUPS, e))
    bc = chunks(bm, (SSM_GROUPS, SSM_STATE))
    cc = chunks(cm, (SSM_GROUPS, SSM_STATE))
    causal = jnp.tril(jnp.ones((CHUNK, CHUNK), dtype=bool))

    def step(h, inp):
        xk, ak, bk, ck = inp
        acs = jnp.cumsum(ak, axis=1)
        seg = acs[:, :, None] - acs[:, None, :]
        decay = jnp.exp(jnp.where(causal[None, :, :, None, None], seg, -jnp.inf))
        y_diag = jnp.einsum('blsg,blsge,bsgep->blgep', jnp.einsum('blgn,bsgn->blsg', ck, bk), decay, xk)
        y_off = jnp.einsum('blgn,bgepn,blge->blgep', ck, h, jnp.exp(acs))
        tail = jnp.exp(acs[:, -1:] - acs)
        h = h * jnp.exp(acs[:, -1])[..., None, None] + jnp.einsum('bsgn,bsge,bsgep->bgepn', bk, tail, xk)
        return h, y_diag + y_off

    h0 = jnp.zeros((b, SSM_GROUPS, e, SSM_HEAD_DIM, SSM_STATE), jnp.float32)
    h, y = lax.scan(step, h0, (xdt, la, bc, cc))
    y = jnp.swapaxes(y, 0, 1).reshape(b, nc * CHUNK, SSM_HEADS, SSM_HEAD_DIM)[:, pad:]
    return y, h.reshape(b, SSM_HEADS, SSM_HEAD_DIM, SSM_STATE)


def ssm_recurrent(x, dt, a, bm, cm, h0):
    e = SSM_HEADS // SSM_GROUPS

    def step(h, inp):
        xt, dtt, bt, ct = inp
        bh = jnp.repeat(bt, e, axis=1)
        ch = jnp.repeat(ct, e, axis=1)
        h = h * jnp.exp(dtt * a)[..., None, None] + jnp.einsum('bh,bhp,bhn->bhpn', dtt, xt, bh)
        return h, jnp.einsum('bhpn,bhn->bhp', h, ch)

    inputs = tuple(jnp.swapaxes(arr.astype(jnp.float32), 0, 1) for arr in (x, dt, bm, cm))
    h, y = lax.scan(step, h0.astype(jnp.float32), inputs)
    return jnp.swapaxes(y, 0, 1), h


def ssm_branch_out(y, x, z, d_skip, ssm_norm):
    b, t = y.shape[:2]
    y = (y + d_skip[:, None].astype(jnp.float32) * x.astype(jnp.float32)).reshape(b, t, SSM_INNER)
    yg = (y * jax.nn.silu(z.astype(jnp.float32))).reshape(b, t, SSM_GROUPS, SSM_INNER // SSM_GROUPS)
    yg = yg * lax.rsqrt(jnp.mean(yg * yg, axis=-1, keepdims=True) + EPS)
    return (yg.reshape(b, t, SSM_INNER) * ssm_norm.astype(jnp.float32)).astype(z.dtype)


def merge_branches(x, o_att, o_ssm, ga, gb, w_attn_out, w_ssm_out, w_out):
    m = (jax.nn.sigmoid(ga) * jnp.einsum('bta,ad->btd', o_att, w_attn_out)
         + jax.nn.sigmoid(gb) * jnp.einsum('bts,sd->btd', o_ssm, w_ssm_out))
    return x + jnp.einsum('btd,de->bte', m, w_out)


def swiglu_ffn(x, g, w_gate, w_up, w_down):
    h = rmsnorm(x, g)
    u = jax.nn.silu(jnp.einsum('btd,df->btf', h, w_gate)) * jnp.einsum('btd,df->btf', h, w_up)
    return x + jnp.einsum('btf,fd->btd', u, w_down)


def setup_inputs(seed: int = 0) -> dict:
    key = jax.random.key(seed)
    ks = jax.random.split(key, 26)
    f32 = jnp.float32
    n_pages = PAST_LEN // PAGE_SIZE
    n_pool = (5 * DEC_BATCH * n_pages) // 4

    def nrm(k, shape, scale=1.0):
        return jax.random.normal(k, shape, f32) * scale

    def gain(k, shape):
        return 1.0 + nrm(k, shape, 0.05)

    page_table = jax.random.permutation(ks[7], n_pool)[: DEC_BATCH * n_pages].reshape(
        DEC_BATCH, n_pages).astype(jnp.int32)
    dt0 = jnp.exp(jax.random.uniform(ks[13], (DEPTH, SSM_HEADS), f32, math.log(DT_MIN), math.log(DT_MAX)))
    return {
        'x_prompt': nrm(ks[0], (BATCH, SEQ, D_MODEL)),
        'x_sample': nrm(ks[1], (DEC_BATCH, DEC_SEQ, D_MODEL)),
        'cache_k': nrm(ks[2], (DEPTH, n_pool, PAGE_SIZE, N_HEADS, HEAD_DIM)),
        'cache_v': nrm(ks[3], (DEPTH, n_pool, PAGE_SIZE, N_HEADS, HEAD_DIM)),
        'cache_idx_k': nrm(ks[4], (DEPTH, n_pool, PAGE_SIZE, IDX_DIM)),
        'state_ssm': nrm(ks[5], (DEPTH, DEC_BATCH, SSM_HEADS, SSM_HEAD_DIM, SSM_STATE), 0.1),
        'state_conv': nrm(ks[6], (DEPTH, DEC_BATCH, CONV_WIDTH - 1, CONV_DIM)),
        'page_table': page_table,
        'meta_tokens': nrm(ks[8], (N_META, D_MODEL)),
        'rel_bias': nrm(ks[9], (N_BUCKETS, N_HEADS), 0.5),
        'norm_mix': gain(ks[10], (DEPTH, D_MODEL)),
        'w_in': nrm(ks[11], (DEPTH, D_MODEL, IN_DIM), D_MODEL ** -0.5),
        'conv_w': nrm(ks[12], (DEPTH, CONV_WIDTH, CONV_DIM), CONV_WIDTH ** -0.5),
        'conv_b': nrm(ks[14], (DEPTH, CONV_DIM), 0.02),
        'dt_bias': dt0 + jnp.log(-jnp.expm1(-dt0)),
        'a_log': jnp.log(jax.random.uniform(ks[15], (DEPTH, SSM_HEADS), f32, 1.0, 16.0)),
        'd_skip': gain(ks[16], (DEPTH, SSM_HEADS)),
        'ssm_norm': gain(ks[17], (DEPTH, SSM_INNER)),
        'w_attn_out': nrm(ks[18], (DEPTH, ATT_DIM, D_MODEL), ATT_DIM ** -0.5),
        'w_ssm_out': nrm(ks[19], (DEPTH, SSM_INNER, D_MODEL), SSM_INNER ** -0.5),
        'w_out': nrm(ks[20], (DEPTH, D_MODEL, D_MODEL), D_MODEL ** -0.5),
        'norm_ffn': gain(ks[21], (DEPTH, D_MODEL)),
        'w_gate': nrm(ks[22], (DEPTH, D_MODEL, FFN_HIDDEN), D_MODEL ** -0.5),
        'w_up': nrm(ks[23], (DEPTH, D_MODEL, FFN_HIDDEN), D_MODEL ** -0.5),
        'w_down': nrm(ks[24], (DEPTH, FFN_HIDDEN, D_MODEL), FFN_HIDDEN ** -0.5),
        'norm_final': gain(ks[25], (D_MODEL,)),
    }


def reference(x_prompt, x_sample, cache_k, cache_v, cache_idx_k, state_ssm, state_conv, page_table,
              meta_tokens, rel_bias, norm_mix, w_in, conv_w, conv_b, dt_bias, a_log, d_skip, ssm_norm,
              w_attn_out, w_ssm_out, w_out, norm_ffn, w_gate, w_up, w_down, norm_final):
    b = x_prompt.shape[0]
    hist = CONV_WIDTH - 1
    meta = jnp.broadcast_to(meta_tokens[None].astype(x_prompt.dtype), (b, N_META, D_MODEL))
    xp = jnp.concatenate([meta, x_prompt], axis=1)
    xs = x_sample
    new_p = []
    new_s = []
    for l in range(DEPTH):
        a = -jnp.exp(a_log[l].astype(jnp.float32))
        h = rmsnorm(xp, norm_mix[l])
        q, k, v, qi, ki, wi, z, xbc, dtr, ga, gb = in_projection(h, w_in[l])
        o_att = prompt_sparse_attention(q, k, v, qi, wi, ki, rel_bias)
        xsm, bm, cm = ssm_conv_split(jnp.pad(xbc, ((0, 0), (hist, 0), (0, 0))), conv_w[l], conv_b[l])
        dt = jax.nn.softplus((dtr + dt_bias[l]).astype(jnp.float32))
        y, h_fin = ssd_chunked(xsm, dt, a, bm, cm)
        o_ssm = ssm_branch_out(y, xsm, z, d_skip[l], ssm_norm[l])
        xp = merge_branches(xp, o_att, o_ssm, ga, gb, w_attn_out[l], w_ssm_out[l], w_out[l])
        xp = swiglu_ffn(xp, norm_ffn[l], w_gate[l], w_up[l], w_down[l])
        new_p.append((k, v, ki, h_fin, xbc[:, -hist:]))
        h = rmsnorm(xs, norm_mix[l])
        q, k, v, qi, ki, wi, z, xbc, dtr, ga, gb = in_projection(h, w_in[l])
        o_att = sample_sparse_attention(q, k, v, qi, wi, ki, cache_k[l], cache_v[l], cache_idx_k[l],
                                        page_table, rel_bias)
        xbc_hist = jnp.concatenate([state_conv[l].astype(xbc.dtype), xbc], axis=1)
        xsm, bm, cm = ssm_conv_split(xbc_hist, conv_w[l], conv_b[l])
        dt = jax.nn.softplus((dtr + dt_bias[l]).astype(jnp.float32))
        y, h_fin = ssm_recurrent(xsm, dt, a, bm, cm, state_ssm[l])
        o_ssm = ssm_branch_out(y, xsm, z, d_skip[l], ssm_norm[l])
        xs = merge_branches(xs, o_att, o_ssm, ga, gb, w_attn_out[l], w_ssm_out[l], w_out[l])
        xs = swiglu_ffn(xs, norm_ffn[l], w_gate[l], w_up[l], w_down[l])
        new_s.append((k, v, ki, h_fin, xbc_hist[:, -hist:]))
    k_prompt, v_prompt, idxk_prompt, ssm_prompt, conv_prompt = [jnp.stack(t) for t in zip(*new_p)]
    k_sample, v_sample, idxk_sample, ssm_sample, conv_sample = [jnp.stack(t) for t in zip(*new_s)]
    y_prompt = rmsnorm(xp, norm_final)[:, N_META:]
    y_sample = rmsnorm(xs, norm_final)
    return (y_prompt, y_sample, k_prompt, v_prompt, idxk_prompt, ssm_prompt, conv_prompt,
            k_sample, v_sample, idxk_sample, ssm_sample, conv_sample)
```

```python
import functools
import math

import jax
import jax.numpy as jnp
import numpy as np
from jax import lax
from jax.experimental import pallas as pl
from jax.experimental.pallas import tpu as pltpu

F32 = jnp.float32
BF16 = jnp.bfloat16
I32 = jnp.int32

N_HEADS = 16
HEAD_DIM = 64
IDX_HEADS = 8
IDX_DIM = 64
TOPK_MAX = 256
MAX_DISTANCE = 128
SSM_HEAD_DIM = 64
SSM_GROUPS = 4
SSM_STATE = 128
CONV_WIDTH = 4
EPS = 1e-6

LANE = 128
Q_TILE = 256
ROW_PAD = 1280
VMEM_LIMIT = 60 * 1024 * 1024
HI = lax.Precision.HIGHEST
INT_MIN = -(2 ** 31)


def _cparams(sem):
    return pltpu.CompilerParams(dimension_semantics=sem, vmem_limit_bytes=VMEM_LIMIT)


def _const_spec(shape):
    nd = len(shape)
    return pl.BlockSpec(shape, lambda *a: (0,) * nd, pipeline_mode=pl.Buffered(1))


def _rms_matmul_kernel(x_ref, g_ref, w_ref, o_ref, h_scr, *, exact):
    @pl.when(pl.program_id(1) == 0)
    def _():
        x = x_ref[...]
        y = x * lax.rsqrt(jnp.mean(x * x, axis=-1, keepdims=True) + EPS) * g_ref[...]
        h_scr[...] = y.astype(h_scr.dtype)

    if exact:
        o_ref[...] = jnp.dot(h_scr[...], w_ref[...], preferred_element_type=F32, precision=HI)
    else:
        o_ref[...] = jnp.dot(h_scr[...], w_ref[...], preferred_element_type=F32).astype(o_ref.dtype)


def rms_matmul(x, g, w, *, tm, tn, exact=False, out_dtype=F32, name):
    r, d = x.shape
    n = w.shape[1]
    assert r % tm == 0 and n % tn == 0
    return pl.pallas_call(
        functools.partial(_rms_matmul_kernel, exact=exact),
        grid=(r // tm, n // tn),
        in_specs=[pl.BlockSpec((tm, d), lambda i, j: (i, 0)),
                  pl.BlockSpec((1, d), lambda i, j: (0, 0)),
                  pl.BlockSpec((d, tn), lambda i, j: (0, j))],
        out_specs=pl.BlockSpec((tm, tn), lambda i, j: (i, j)),
        out_shape=jax.ShapeDtypeStruct((r, n), out_dtype),
        scratch_shapes=[pltpu.VMEM((tm, d), F32 if exact else BF16)],
        compiler_params=_cparams(("parallel", "arbitrary")),
        name=name,
    )(x, g.reshape(1, d), w)


def _sortable(x):
    bits = pltpu.bitcast(x + 0.0, I32)
    return bits ^ ((bits >> 31) & 0x7FFFFFFF)


def _split_bf16(x):
    hi = x.astype(BF16)
    lo = (x - hi.astype(F32)).astype(BF16)
    return hi, lo


def _prompt_attn_kernel(q_ref, qi_ref, kw_ref, ki_ref, kT_ref, vT_ref, tab_ref, o_ref,
                        key_scr, lhs_scr, wb_scr, thr_scr, need_scr, carry_scr,
                        qh_scr, m_scr, l_scr, acc_scr, *, n_sub_per_tile, topk):
    bq = Q_TILE
    i = pl.program_id(0)
    j = pl.program_id(1)
    nkt = pl.num_programs(1)
    row_iota = lax.broadcasted_iota(I32, (bq, bq), 0)
    col_iota = lax.broadcasted_iota(I32, (bq, bq), 1)

    @pl.when(j == 0)
    def _select():
        qi = qi_ref[...] * (IDX_DIM ** -0.5)
        hi, lo = _split_bf16(qi)
        for h in range(IDX_HEADS):
            sl = slice(h * IDX_DIM, (h + 1) * IDX_DIM)
            lhs_scr[pl.ds(h * bq, bq), :] = jnp.concatenate(
                [hi[:, sl], lo[:, sl], hi[:, sl], lo[:, sl]], axis=1)
        w = kw_ref[...][:, IDX_DIM:IDX_DIM + IDX_HEADS] * (IDX_HEADS ** -0.5)
        for h in range(IDX_HEADS):
            wb_scr[h] = jnp.broadcast_to(w[:, h:h + 1], (bq, bq))
        for h in range(N_HEADS):
            qh_scr[h] = (q_ref[:, h * HEAD_DIM:(h + 1) * HEAD_DIM] * (HEAD_DIM ** -0.5)).astype(BF16)
        m_scr[...] = jnp.full(m_scr.shape, -1e30, F32)
        l_scr[...] = jnp.zeros(l_scr.shape, F32)
        acc_scr[...] = jnp.zeros(acc_scr.shape, F32)
        carry_scr[...] = jnp.zeros(carry_scr.shape, F32)

        def score_body(c, _):
            khi, klo = _split_bf16(ki_ref[c])
            rhs = jnp.concatenate([khi, khi, klo, klo], axis=1)
            s = lax.dot_general(lhs_scr[...], rhs, (((1,), (1,)), ((), ())),
                                preferred_element_type=F32)
            tot = jnp.zeros((bq, bq), F32)
            for h in range(IDX_HEADS):
                tot = tot + wb_scr[h] * jnp.maximum(s[h * bq:(h + 1) * bq], 0.0)
            causal = (c * bq + col_iota) <= (i * bq + row_iota)
            tot = jnp.where(causal, tot, -jnp.inf)
            key_scr[c] = _sortable(tot)
            return 0

        lax.fori_loop(0, i + 1, score_body, 0)

        def count_ge(cand):
            cand_b = jnp.broadcast_to(cand, (bq, bq))

            def body(c, cnt):
                one = jnp.where(key_scr[c] >= cand_b, 1.0, 0.0)
                return cnt + one[:, :LANE] + one[:, LANE:]

            cnt = lax.fori_loop(0, i + 1, body, jnp.zeros((bq, LANE), F32))
            return jnp.sum(cnt, axis=1, keepdims=True)

        def bit_body(b, thr):
            cand = thr ^ jnp.left_shift(jnp.int32(1), 31 - b)
            return jnp.where(count_ge(cand) >= topk, cand, thr)

        thr = lax.fori_loop(0, 32, bit_body, jnp.full((bq, 1), INT_MIN, I32))
        n_gt = count_ge(thr + 1)
        thr_scr[...] = jnp.broadcast_to(thr, (bq, bq))
        need_scr[...] = jnp.broadcast_to(topk - n_gt, (bq, bq))

    def attend(s, near):
        c = j * n_sub_per_tile + s
        key = key_scr[c]
        thr = thr_scr[...]
        eq = key == thr
        eqf = jnp.where(eq, 1.0, 0.0).astype(BF16)
        tri = jnp.where(row_iota <= col_iota, 1.0, 0.0).astype(BF16)
        rank = jnp.dot(eqf, tri, preferred_element_type=F32) + carry_scr[...]
        carry_scr[...] = jnp.broadcast_to(rank[:, bq - 1:bq], (bq, bq))
        take_eq = jnp.where(eq, jnp.where(rank <= need_scr[...], 1.0, 0.0), 0.0)
        take = jnp.where(key > thr, 1.0, take_eq)
        causal = (c * bq + col_iota) <= (i * bq + row_iota)
        sel = jnp.where(causal, take, 0.0) > 0.5
        lanes = pl.ds(s * bq, bq)

        def head_body(h, _):
            rows = pl.ds(pl.multiple_of(h * HEAD_DIM, HEAD_DIM), HEAD_DIM)
            logit = jnp.dot(qh_scr[h], kT_ref[rows, lanes], preferred_element_type=F32)
            if near:
                logit = logit + tab_ref[i - c, h]
            logit = jnp.where(sel, logit, -jnp.inf)
            m_old = m_scr[h]
            m_new = jnp.maximum(m_old, jnp.max(logit, axis=1, keepdims=True))
            alpha = jnp.exp(m_old - m_new)
            p = jnp.exp(logit - m_new[:, :1])
            l_scr[h] = alpha * l_scr[h] + jnp.sum(p, axis=1, keepdims=True)
            pv = lax.dot_general(p.astype(BF16), vT_ref[rows, lanes], (((1,), (1,)), ((), ())),
                                 preferred_element_type=F32)
            acc_scr[h] = acc_scr[h] * alpha[:, :HEAD_DIM] + pv
            m_scr[h] = m_new
            return 0

        lax.fori_loop(0, N_HEADS, head_body, 0)

    for s in range(n_sub_per_tile):
        c = j * n_sub_per_tile + s

        @pl.when(c <= i - 2)
        def _far(s=s):
            attend(s, False)

        @pl.when((c > i - 2) & (c <= i))
        def _near(s=s):
            attend(s, True)

    @pl.when(j == nkt - 1)
    def _finish():
        for h in range(N_HEADS):
            o_ref[:, h * HEAD_DIM:(h + 1) * HEAD_DIM] = (
                acc_scr[h] / l_scr[h][:, :HEAD_DIM]).astype(o_ref.dtype)


def _t5_bucket(dist, n_buckets):
    max_exact = n_buckets // 2
    d = jnp.maximum(dist, 0)
    df = jnp.maximum(d, 1).astype(F32)
    large = max_exact + (jnp.log(df / max_exact) / math.log(MAX_DISTANCE / max_exact)
                         * (n_buckets - max_exact)).astype(I32)
    return jnp.where(d < max_exact, d, jnp.minimum(large, n_buckets - 1))


def prompt_attention(q, qidx, ki, kT, vT, rel_bias, *, t_real):
    tp = q.shape[0]
    bq = Q_TILE
    nq = tp // bq
    kt = ROW_PAD if tp % ROW_PAD == 0 else bq
    nsub = kt // bq
    nkt = tp // kt
    topk = min(TOPK_MAX, t_real // 4)
    n_buckets = rel_bias.shape[0]
    assert 2 * bq - (bq - 1) >= MAX_DISTANCE
    qo = jnp.arange(bq, dtype=I32)[:, None]
    ko = jnp.arange(bq, dtype=I32)[None, :]
    dist = jnp.stack([qo - ko, bq + qo - ko])
    tab = rel_bias[_t5_bucket(dist, n_buckets)] - rel_bias[n_buckets - 1]
    tab = jnp.moveaxis(tab, -1, 1).astype(F32)

    def kv_map(i, j):
        return (0, jnp.minimum(j, ((i + 1) * bq - 1) // kt))

    kernel = functools.partial(_prompt_attn_kernel, n_sub_per_tile=nsub, topk=topk)
    return pl.pallas_call(
        kernel,
        grid=(nq, nkt),
        in_specs=[pl.BlockSpec((bq, N_HEADS * HEAD_DIM), lambda i, j: (i, 0)),
                  pl.BlockSpec((bq, IDX_HEADS * IDX_DIM), lambda i, j: (i, 0)),
                  pl.BlockSpec((bq, LANE), lambda i, j: (i, IDX_HEADS * IDX_DIM // LANE)),
                  _const_spec((nq, bq, IDX_DIM)),
                  pl.BlockSpec((N_HEADS * HEAD_DIM, kt), kv_map),
                  pl.BlockSpec((N_HEADS * HEAD_DIM, kt), kv_map),
                  _const_spec((2, N_HEADS, bq, bq))],
        out_specs=pl.BlockSpec((bq, N_HEADS * HEAD_DIM), lambda i, j: (i, 0)),
        out_shape=jax.ShapeDtypeStruct((tp, N_HEADS * HEAD_DIM), BF16),
        scratch_shapes=[pltpu.VMEM((nq, bq, bq), I32),
                        pltpu.VMEM((IDX_HEADS * bq, 4 * IDX_DIM), BF16),
                        pltpu.VMEM((IDX_HEADS, bq, bq), F32),
                        pltpu.VMEM((bq, bq), I32),
                        pltpu.VMEM((bq, bq), F32),
                        pltpu.VMEM((bq, bq), F32),
                        pltpu.VMEM((N_HEADS, bq, HEAD_DIM), BF16),
                        pltpu.VMEM((N_HEADS, bq, LANE), F32),
                        pltpu.VMEM((N_HEADS, bq, LANE), F32),
                        pltpu.VMEM((N_HEADS, bq, HEAD_DIM), F32)],
        compiler_params=_cparams(("arbitrary", "arbitrary")),
        name="prompt_attn",
    )(q, qidx, qidx, ki.reshape(nq, bq, IDX_DIM), kT, vT, tab)


SSD_CHUNK = 128
CONV_CARRY = 8


def _silu(x):
    return x * jax.nn.sigmoid(x)


def _softplus(x):
    return jnp.maximum(x, 0.0) + jnp.log1p(jnp.exp(-jnp.abs(x)))


def _gated_group_norm(y, xs, z, dskip, norm):
    yg = (y + dskip * xs) * _silu(z)
    gw = yg.shape[-1] // SSM_GROUPS
    outs = []
    for g in range(SSM_GROUPS):
        part = yg[:, g * gw:(g + 1) * gw]
        ms = jnp.mean(part * part, axis=-1, keepdims=True)
        outs.append(part * lax.rsqrt(ms + EPS))
    return jnp.concatenate(outs, axis=-1) * norm


def _ssd_kernel(xbc_ref, z_ref, dtr_ref, convw_ref, convb_ref, dtb_ref, alog_ref, dskip_ref, norm_ref,
                o_ref, hfin_ref, tail_scr, ht_scr, y_scr, *, t_real, n_heads, inner):
    L = SSD_CHUNK
    P = SSM_HEAD_DIM
    N = SSM_STATE
    e = n_heads // SSM_GROUPS
    ci = pl.program_id(0)

    @pl.when(ci == 0)
    def _():
        tail_scr[...] = jnp.zeros(tail_scr.shape, F32)
        ht_scr[...] = jnp.zeros(ht_scr.shape, F32)

    xbc = xbc_ref[...]
    hist = jnp.concatenate([tail_scr[...], xbc], axis=0)
    conv = convb_ref[...] + convw_ref[CONV_WIDTH - 1:CONV_WIDTH, :] * xbc
    for jx in range(CONV_WIDTH - 1):
        shifted = pltpu.roll(hist, CONV_WIDTH - 1 - jx, axis=0)[CONV_CARRY:]
        conv = conv + convw_ref[jx:jx + 1, :] * shifted
    tail_scr[...] = xbc[L - CONV_CARRY:, :]
    act = _silu(conv)
    xs = act[:, :inner]
    bm = act[:, inner:inner + SSM_GROUPS * N]
    cm = act[:, inner + SSM_GROUPS * N:]

    row = lax.broadcasted_iota(I32, (L, LANE), 0)
    dt = _softplus(dtr_ref[...] + dtb_ref[...])
    dt = jnp.where(ci * L + row < t_real, dt, 0.0)
    la = dt * (-jnp.exp(alog_ref[...]))
    r2 = lax.broadcasted_iota(I32, (L, L), 0)
    c2 = lax.broadcasted_iota(I32, (L, L), 1)
    causal = r2 >= c2
    acs = jnp.dot(jnp.where(causal, 1.0, 0.0), la, preferred_element_type=F32, precision=HI)
    acs_t = acs.T
    acs_last = acs[L - 1:L, :]
    dec_last = jnp.exp(acs_last)

    for g in range(SSM_GROUPS):
        bg = bm[:, g * N:(g + 1) * N]
        cg = cm[:, g * N:(g + 1) * N]
        cg16 = cg.astype(BF16)
        cb = lax.dot_general(cg16, bg.astype(BF16), (((1,), (1,)), ((), ())),
                             preferred_element_type=F32)
        bg_t = bg.T
        for hh in range(e):
            h = g * e + hh
            a_col = acs[:, h:h + 1]
            a_row = acs_t[h:h + 1, :]
            decay = jnp.exp(jnp.where(causal, a_col - a_row, -jnp.inf))
            xdt = (xs[:, h * P:(h + 1) * P] * dt[:, h:h + 1]).astype(BF16)
            y = jnp.dot((cb * decay).astype(BF16), xdt, preferred_element_type=F32)
            ht = ht_scr[h]
            y = y + jnp.dot(cg16, ht.astype(BF16), preferred_element_type=F32) * jnp.exp(a_col)
            y_scr[:, h * P:(h + 1) * P] = y
            tail = jnp.exp(acs_last[:, h:h + 1] - a_row)
            ht_scr[h] = ht * dec_last[:, h:h + 1] + jnp.dot(
                (bg_t * tail).astype(BF16), xdt, preferred_element_type=F32)

    o_ref[...] = _gated_group_norm(y_scr[...], xs, z_ref[...], dskip_ref[...],
                                   norm_ref[...]).astype(o_ref.dtype)

    @pl.when(ci == pl.num_programs(0) - 1)
    def _():
        hfin_ref[...] = ht_scr[...]


def _pad_lanes(v, n=LANE):
    return jnp.pad(v, (0, n - v.shape[0])).reshape(1, n)


def ssd_prompt(xbc, z, idx, conv_w, conv_b, dt_bias, a_log, d_skip, ssm_norm, *, t_real):
    tp, conv_dim = xbc.shape
    inner = z.shape[1]
    n_heads = a_log.shape[0]
    L = SSD_CHUNK
    kernel = functools.partial(_ssd_kernel, t_real=t_real, n_heads=n_heads, inner=inner)
    dtr_block = idx.shape[1] // LANE - 1
    return pl.pallas_call(
        kernel,
        grid=(tp // L,),
        in_specs=[pl.BlockSpec((L, conv_dim), lambda c: (c, 0)),
                  pl.BlockSpec((L, inner), lambda c: (c, 0)),
                  pl.BlockSpec((L, LANE), lambda c: (c, dtr_block)),
                  _const_spec((CONV_WIDTH, conv_dim)),
                  _const_spec((1, conv_dim)),
                  _const_spec((1, LANE)),
                  _const_spec((1, LANE)),
                  _const_spec((1, inner)),
                  _const_spec((1, inner))],
        out_specs=[pl.BlockSpec((L, inner), lambda c: (c, 0)),
                   pl.BlockSpec((n_heads, SSM_STATE, SSM_HEAD_DIM), lambda c: (0, 0, 0))],
        out_shape=[jax.ShapeDtypeStruct((tp, inner), BF16),
                   jax.ShapeDtypeStruct((n_heads, SSM_STATE, SSM_HEAD_DIM), F32)],
        scratch_shapes=[pltpu.VMEM((CONV_CARRY, conv_dim), F32),
                        pltpu.VMEM((n_heads, SSM_STATE, SSM_HEAD_DIM), F32),
                        pltpu.VMEM((L, inner), F32)],
        compiler_params=_cparams(("arbitrary",)),
        name="ssd_prompt",
    )(xbc, z, idx, conv_w, conv_b.reshape(1, conv_dim), _pad_lanes(dt_bias), _pad_lanes(a_log),
      jnp.repeat(d_skip, SSM_HEAD_DIM).reshape(1, inner), ssm_norm.reshape(1, inner))


def _rms(x, g):
    return x * lax.rsqrt(jnp.mean(x * x, axis=-1, keepdims=True) + EPS) * g


def _merge_ffn_kernel(x_ref, oa_ref, os_ref, gate_ref, wa_ref, ws_ref, wo_ref, nf_ref, wg_ref, wu_ref,
                      wd_ref, nfin_ref, y_ref):
    d = x_ref.shape[-1]
    dot = functools.partial(jnp.dot, preferred_element_type=F32)
    ga = jax.nn.sigmoid(gate_ref[:, :d])
    gb = jax.nn.sigmoid(gate_ref[:, d:])
    m = ga * dot(oa_ref[...].astype(BF16), wa_ref[...]) + gb * dot(os_ref[...].astype(BF16), ws_ref[...])
    x1 = x_ref[...] + dot(m.astype(BF16), wo_ref[...])
    h = _rms(x1, nf_ref[...]).astype(BF16)
    u = _silu(dot(h, wg_ref[...])) * dot(h, wu_ref[...])
    x2 = x1 + dot(u.astype(BF16), wd_ref[...])
    y_ref[...] = _rms(x2, nfin_ref[...])


def merge_ffn(x, o_att, o_ssm, gates, w_attn_out, w_ssm_out, w_out, norm_ffn, w_gate, w_up, w_down,
              norm_final, *, tm, name):
    r, d = x.shape
    row = lambda width: pl.BlockSpec((tm, width), lambda i: (i, 0))
    ws = [w_attn_out.astype(BF16), w_ssm_out.astype(BF16), w_out.astype(BF16), norm_ffn.reshape(1, d),
          w_gate.astype(BF16), w_up.astype(BF16), w_down.astype(BF16), norm_final.reshape(1, d)]
    return pl.pallas_call(
        _merge_ffn_kernel,
        grid=(r // tm,),
        in_specs=[row(d), row(o_att.shape[1]), row(o_ssm.shape[1]), row(gates.shape[1])]
        + [_const_spec(w.shape) for w in ws],
        out_specs=row(d),
        out_shape=jax.ShapeDtypeStruct((r, d), F32),
        compiler_params=_cparams(("parallel",)),
        name=name,
    )(x, o_att, o_ssm, gates, *ws)


PAGES_PER_STEP = 16


def _sample_select_kernel(pt_ref, qi_ref, w_ref, knew_ref, *rest, n_pages, page, topk):
    page_refs = rest[:PAGES_PER_STEP]
    rows_ref, pos_ref, sc_scr, pre_scr = rest[PAGES_PER_STEP:]
    b = pl.program_id(0)
    g = pl.program_id(1)
    qi = qi_ref[...] * (IDX_DIM ** -0.5)
    w = w_ref[...] * (IDX_HEADS ** -0.5)

    for p in range(PAGES_PER_STEP):
        s = lax.dot_general(qi, page_refs[p][...], (((1,), (1,)), ((), ())),
                            preferred_element_type=F32, precision=HI)
        sc_scr[pl.ds(g * PAGES_PER_STEP + p, 1), :] = jnp.sum(
            w * jnp.maximum(s, 0.0), axis=0, keepdims=True)

    @pl.when(g == pl.num_programs(1) - 1)
    def _():
        past = n_pages * page
        s_new = jnp.sum(qi * knew_ref[...], axis=1, keepdims=True)
        s_new = jnp.sum(w * jnp.maximum(s_new, 0.0), axis=0, keepdims=True)
        keys = _sortable(sc_scr[...])
        key_new = _sortable(s_new)

        def total(x):
            return jnp.sum(jnp.sum(x, axis=1, keepdims=True), axis=0, keepdims=True)

        def count_ge(cand):
            return (total(jnp.where(keys >= cand, 1.0, 0.0))
                    + jnp.where(key_new >= cand, 1.0, 0.0))

        def bit_body(bit, thr):
            cand = thr ^ jnp.left_shift(jnp.int32(1), 31 - bit)
            return jnp.where(count_ge(cand) >= topk, cand, thr)

        thr = lax.fori_loop(0, 32, bit_body, jnp.full((1, 1), INT_MIN, I32))
        need = topk - count_ge(thr + 1)

        tri = jnp.where(lax.broadcasted_iota(I32, (page, page), 0)
                        <= lax.broadcasted_iota(I32, (page, page), 1), 1.0, 0.0).astype(BF16)
        low = jnp.where(lax.broadcasted_iota(I32, (n_pages, n_pages), 1)
                        < lax.broadcasted_iota(I32, (n_pages, n_pages), 0), 1.0, 0.0).astype(BF16)

        def prefix(mask):
            mb = mask.astype(BF16)
            within = jnp.dot(mb, tri, preferred_element_type=F32)
            tot = jnp.broadcast_to(within[:, page - 1:page], (n_pages, page))
            return within + jnp.dot(low, tot.astype(BF16), preferred_element_type=F32)

        eq = jnp.where(keys == thr, 1.0, 0.0)
        take = jnp.where(keys > thr, 1.0, jnp.where(prefix(eq) <= need, eq, 0.0))
        n_cached = total(take)
        pre_scr[...] = prefix(take) * take

        r1 = lax.broadcasted_iota(I32, (topk, page), 0).astype(F32) + 1.0
        slot = lax.broadcasted_iota(I32, (1, page), 1)

        def gather_body(p, carry):
            acc_rows, acc_pos = carry
            hit = jnp.broadcast_to(pre_scr[pl.ds(p, 1), :], (topk, page)) == r1
            phys = (pt_ref[b, p] * page + slot).astype(F32)
            logical = (p * page + slot).astype(F32)
            return (acc_rows + jnp.where(hit, phys, 0.0), acc_pos + jnp.where(hit, logical, 0.0))

        zero = jnp.zeros((topk, page), F32)
        acc_rows, acc_pos = lax.fori_loop(0, n_pages, gather_body, (zero, zero))
        rows = jnp.sum(acc_rows, axis=1, keepdims=True)
        pos = jnp.sum(acc_pos, axis=1, keepdims=True)
        last = lax.broadcasted_iota(I32, (topk, 1), 0).astype(F32) >= n_cached
        rows_ref[...] = jnp.where(last, -1.0, rows).astype(I32)
        pos_ref[...] = jnp.where(last, float(past), pos).astype(I32)


def sample_select(qi, w, k_new, cache_idx, page_table, *, topk):
    bsz, n_pages = page_table.shape
    page = cache_idx.shape[1]
    assert n_pages % PAGES_PER_STEP == 0 and n_pages * page * page < 2 ** 24
    ng = n_pages // PAGES_PER_STEP
    kernel = functools.partial(_sample_select_kernel, n_pages=n_pages, page=page, topk=topk)

    def page_spec(p):
        return pl.BlockSpec((None, page, IDX_DIM),
                            lambda b, g, pt: (pt[b, g * PAGES_PER_STEP + p], 0, 0))

    out_spec = pl.BlockSpec((None, topk, 1), lambda b, g, pt: (b, 0, 0))
    return pl.pallas_call(
        kernel,
        grid_spec=pltpu.PrefetchScalarGridSpec(
            num_scalar_prefetch=1,
            grid=(bsz, ng),
            in_specs=[pl.BlockSpec((None, IDX_HEADS, IDX_DIM), lambda b, g, pt: (b, 0, 0)),
                      pl.BlockSpec((None, IDX_HEADS, 1), lambda b, g, pt: (b, 0, 0)),
                      pl.BlockSpec((None, 1, IDX_DIM), lambda b, g, pt: (b, 0, 0))]
            + [page_spec(p) for p in range(PAGES_PER_STEP)],
            out_specs=[out_spec, out_spec],
            scratch_shapes=[pltpu.VMEM((n_pages, page), F32),
                            pltpu.VMEM((n_pages, page), F32)]),
        out_shape=[jax.ShapeDtypeStruct((bsz, topk, 1), I32)] * 2,
        compiler_params=_cparams(("arbitrary", "arbitrary")),
        name="sample_select",
    )(page_table, qi, w, k_new, *([cache_idx] * PAGES_PER_STEP))


def _sample_attn_kernel(rows_ref, q_ref, pos_ref, knew_ref, vnew_ref, bias_ref, ck_ref, cv_ref, o_ref,
                        kbuf, vbuf, sem, *, topk, past, n_buckets):
    b = pl.program_id(0)
    nb = pl.num_programs(0)

    def row_copies(seq, slot, r, row):
        return (pltpu.make_async_copy(ck_ref.at[pl.ds(row, 1)], kbuf.at[slot, pl.ds(r, 1)], sem.at[0, slot]),
                pltpu.make_async_copy(cv_ref.at[pl.ds(row, 1)], vbuf.at[slot, pl.ds(r, 1)], sem.at[1, slot]))

    def new_copies(seq, slot, r):
        return (pltpu.make_async_copy(knew_ref.at[pl.ds(seq, 1)], kbuf.at[slot, pl.ds(r, 1)], sem.at[0, slot]),
                pltpu.make_async_copy(vnew_ref.at[pl.ds(seq, 1)], vbuf.at[slot, pl.ds(r, 1)], sem.at[1, slot]))

    def issue(seq, slot):
        def body(r, _):
            for cp in row_copies(seq, slot, r, rows_ref[seq, r]):
                cp.start()
            return 0

        lax.fori_loop(0, topk - 1, body, 0)
        last = rows_ref[seq, topk - 1]

        @pl.when(last >= 0)
        def _():
            for cp in row_copies(seq, slot, topk - 1, last):
                cp.start()

        @pl.when(last < 0)
        def _():
            for cp in new_copies(seq, slot, topk - 1):
                cp.start()

    def wait(slot):
        def body(r, _):
            for cp in row_copies(0, slot, r, 0):
                cp.wait()
            return 0

        lax.fori_loop(0, topk, body, 0)

    slot = b % 2

    @pl.when(b == 0)
    def _():
        issue(0, 0)

    @pl.when(b + 1 < nb)
    def _():
        issue(b + 1, 1 - slot)

    wait(slot)

    att = q_ref.shape[-1]
    head_of = lax.broadcasted_iota(I32, (att, LANE), 0) // HEAD_DIM
    seg = jnp.where(head_of == lax.broadcasted_iota(I32, (att, LANE), 1), 1.0, 0.0).astype(BF16)
    head_of_t = lax.broadcasted_iota(I32, (LANE, att), 1) // HEAD_DIM
    seg_t = jnp.where(head_of_t == lax.broadcasted_iota(I32, (LANE, att), 0), 1.0, 0.0).astype(BF16)
    dot = functools.partial(jnp.dot, preferred_element_type=F32)

    prod = kbuf[slot] * q_ref[...]
    p_hi, p_lo = _split_bf16(prod)
    logits = (dot(p_hi, seg) + dot(p_lo, seg)) * (HEAD_DIM ** -0.5)
    bucket = _t5_bucket(past - pos_ref[...], n_buckets)
    onehot = jnp.where(bucket == lax.broadcasted_iota(I32, (topk, LANE), 1), 1.0, 0.0)
    logits = logits + jnp.dot(onehot, bias_ref[...], preferred_element_type=F32, precision=HI)
    m = jnp.max(logits, axis=0, keepdims=True)
    p = jnp.exp(logits - m)
    p = p / jnp.sum(p, axis=0, keepdims=True)
    w_hi, w_lo = _split_bf16(p)
    pe = dot(w_hi, seg_t) + dot(w_lo, seg_t)
    o_ref[...] = jnp.sum(pe * vbuf[slot], axis=0, keepdims=True)


def sample_attention(q, rows, pos, k_new, v_new, rel_bias, cache_k, cache_v, *, past):
    bsz, _, att = q.shape
    topk = rows.shape[1]
    n_buckets, n_heads = rel_bias.shape
    assert n_buckets <= LANE and n_heads == N_HEADS
    rel_bias = jnp.pad(rel_bias, ((0, LANE - n_buckets), (0, LANE - n_heads)))
    kernel = functools.partial(_sample_attn_kernel, topk=topk, past=past, n_buckets=n_buckets)
    any_spec = pl.BlockSpec(memory_space=pl.ANY)
    return pl.pallas_call(
        kernel,
        grid_spec=pltpu.PrefetchScalarGridSpec(
            num_scalar_prefetch=1,
            grid=(bsz,),
            in_specs=[pl.BlockSpec((None, 1, att), lambda b, r: (b, 0, 0)),
                      pl.BlockSpec((None, topk, 1), lambda b, r: (b, 0, 0)),
                      any_spec, any_spec,
                      pl.BlockSpec(rel_bias.shape, lambda b, r: (0, 0)),
                      any_spec, any_spec],
            out_specs=pl.BlockSpec((None, 1, att), lambda b, r: (b, 0, 0)),
            scratch_shapes=[pltpu.VMEM((2, topk, att), F32),
                            pltpu.VMEM((2, topk, att), F32),
                            pltpu.SemaphoreType.DMA((2, 2))]),
        out_shape=jax.ShapeDtypeStruct((bsz, 1, att), F32),
        compiler_params=_cparams(("arbitrary",)),
        name="sample_attn",
    )(rows, q, pos, k_new, v_new, rel_bias, cache_k, cache_v)


def _sample_ssm_kernel(xbc_ref, cst_ref, z_ref, dtr_ref, h0_ref, convw_ref, convb_ref, dtb_ref, alog_ref,
                       dskip_ref, norm_ref, o_ref, h_ref, *, inner):
    N = SSM_STATE
    gw = inner // SSM_GROUPS
    conv = convb_ref[...] + convw_ref[CONV_WIDTH - 1:CONV_WIDTH, :] * xbc_ref[...]
    for jx in range(CONV_WIDTH - 1):
        conv = conv + convw_ref[jx:jx + 1, :] * cst_ref[jx:jx + 1, :]
    act = _silu(conv)
    xs = act[:, :inner]
    dt = _softplus(dtr_ref[...] + dtb_ref[...])
    decay = jnp.exp(dt * (-jnp.exp(alog_ref[...])))
    dtx = dt * xs

    def column_bcast(rowvec):
        return jnp.broadcast_to(rowvec, (LANE, inner)).T

    dec_c = column_bcast(decay)
    dtx_c = column_bcast(dtx)
    ys = []
    for g in range(SSM_GROUPS):
        rs = slice(g * gw, (g + 1) * gw)
        bg = act[:, inner + g * N:inner + (g + 1) * N]
        cg = act[:, inner + (SSM_GROUPS + g) * N:inner + (SSM_GROUPS + g + 1) * N]
        hn = h0_ref[rs, :] * dec_c[rs, :] + dtx_c[rs, :] * bg
        h_ref[rs, :] = hn
        yc = jnp.sum(hn * cg, axis=1, keepdims=True)
        ys.append(jnp.broadcast_to(yc, (gw, LANE)).T[0:1, :])
    y = jnp.concatenate(ys, axis=1)
    o_ref[...] = _gated_group_norm(y, xs, z_ref[...], dskip_ref[...], norm_ref[...])


def sample_ssm(xbc, state_conv, z, dtr_x, state, conv_w, conv_b, dt_bias, a_log, d_skip, ssm_norm):
    bsz, _, conv_dim = xbc.shape
    inner = z.shape[-1]
    rep = lambda v: jnp.repeat(v, SSM_HEAD_DIM).reshape(1, inner)
    per_b = lambda shape: pl.BlockSpec((None,) + shape, lambda b: (b, 0, 0))
    kernel = functools.partial(_sample_ssm_kernel, inner=inner)
    return pl.pallas_call(
        kernel,
        grid=(bsz,),
        in_specs=[per_b((1, conv_dim)), per_b((CONV_WIDTH - 1, conv_dim)), per_b((1, inner)),
                  per_b((1, inner)), per_b((inner, SSM_STATE)),
                  _const_spec((CONV_WIDTH, conv_dim)), _const_spec((1, conv_dim)),
                  _const_spec((1, inner)), _const_spec((1, inner)), _const_spec((1, inner)),
                  _const_spec((1, inner))],
        out_specs=[per_b((1, inner)), per_b((inner, SSM_STATE))],
        out_shape=[jax.ShapeDtypeStruct((bsz, 1, inner), F32),
                   jax.ShapeDtypeStruct((bsz, inner, SSM_STATE), F32)],
        compiler_params=_cparams(("parallel",)),
        name="sample_ssm",
    )(xbc, state_conv, z, dtr_x, state, conv_w, conv_b.reshape(1, conv_dim), rep(dt_bias),
      rep(a_log), rep(d_skip), ssm_norm.reshape(1, inner))


def _split_w_in(w_in, d_model, ssm_inner, conv_dim, ssm_heads):
    att = N_HEADS * HEAD_DIM
    sizes = (att, att, att, IDX_HEADS * IDX_DIM, IDX_DIM, IDX_HEADS, ssm_inner, conv_dim, ssm_heads,
             d_model, d_model)
    pts = [0] + [int(v) for v in np.cumsum(sizes)]
    seg = lambda a, b: w_in[:, pts[a]:pts[b]]
    w_qkv = seg(0, 3).astype(BF16)
    zeros = lambda n: jnp.zeros((w_in.shape[0], n), w_in.dtype)
    w_idx = jnp.concatenate([seg(3, 4), seg(4, 6), zeros(LANE - IDX_DIM - IDX_HEADS),
                             seg(8, 9), zeros(LANE - ssm_heads)], axis=1)
    w_z = seg(6, 7).astype(BF16)
    w_xbc = seg(7, 8).astype(BF16)
    w_gate = seg(9, 11).astype(BF16)
    w_dtx = jnp.repeat(seg(8, 9), SSM_HEAD_DIM, axis=1)
    return w_qkv, w_idx, w_z, w_xbc, w_gate, w_dtx


def kernel(x_prompt, x_sample, cache_k, cache_v, cache_idx_k, state_ssm, state_conv, page_table,
           meta_tokens, rel_bias, norm_mix, w_in, conv_w, conv_b, dt_bias, a_log, d_skip, ssm_norm,
           w_attn_out, w_ssm_out, w_out, norm_ffn, w_gate, w_up, w_down, norm_final):
    assert w_in.shape[0] == 1 and x_prompt.shape[0] == 1 and x_sample.shape[1] == 1
    d_model = x_prompt.shape[-1]
    n_meta = meta_tokens.shape[0]
    t_real = n_meta + x_prompt.shape[1]
    tp = -(-t_real // ROW_PAD) * ROW_PAD if t_real > ROW_PAD else -(-t_real // Q_TILE) * Q_TILE
    bsz = x_sample.shape[0]
    n_pool, page = cache_k.shape[1:3]
    past = page_table.shape[1] * page
    inner = ssm_norm.shape[-1]
    conv_dim = conv_w.shape[-1]
    n_ssm_heads = a_log.shape[-1]
    att = N_HEADS * HEAD_DIM
    i0 = IDX_HEADS * IDX_DIM
    l = 0
    w_qkv, w_idx, w_z, w_xbc, w_g, w_dtx = _split_w_in(w_in[l], d_model, inner, conv_dim, n_ssm_heads)
    ssm_params = (conv_w[l], conv_b[l], dt_bias[l], a_log[l], d_skip[l], ssm_norm[l])
    out_params = (w_attn_out[l], w_ssm_out[l], w_out[l], norm_ffn[l], w_gate[l], w_up[l], w_down[l],
                  norm_final)

    def project(x, tm, tag):
        proj = functools.partial(rms_matmul, x, norm_mix[l], tm=tm)
        return (proj(w_qkv, tn=1024, name=tag + "_qkv"),
                proj(w_idx, tn=w_idx.shape[1], exact=True, name=tag + "_idx"),
                proj(w_z, tn=1024, name=tag + "_z"),
                proj(w_xbc, tn=1024, name=tag + "_xbc"),
                proj(w_g, tn=1024, name=tag + "_gates"))

    tm_p = 640 if tp % 640 == 0 else Q_TILE
    xp = jnp.concatenate([meta_tokens.astype(x_prompt.dtype), x_prompt[0],
                          jnp.zeros((tp - t_real, d_model), x_prompt.dtype)], axis=0)
    qkv, idx, z, xbc, gates = project(xp, tm_p, "prompt")
    kT = qkv[:, att:2 * att].T.astype(BF16)
    vT = qkv[:, 2 * att:].T.astype(BF16)
    o_att = prompt_attention(qkv, idx, idx[:, i0:i0 + IDX_DIM], kT, vT, rel_bias, t_real=t_real)
    o_ssm, h_t = ssd_prompt(xbc, z, idx, *ssm_params, t_real=t_real)
    y = merge_ffn(xp, o_att, o_ssm, gates, *out_params, tm=Q_TILE, name="prompt_merge_ffn")
    y_prompt = y[n_meta:t_real][None]
    k_prompt = qkv[:t_real, att:2 * att].reshape(1, 1, t_real, N_HEADS, HEAD_DIM)
    v_prompt = qkv[:t_real, 2 * att:].reshape(1, 1, t_real, N_HEADS, HEAD_DIM)
    idxk_prompt = idx[:t_real, i0:i0 + IDX_DIM][None, None]
    ssm_prompt = jnp.swapaxes(h_t, 1, 2)[None, None]
    conv_prompt = xbc[t_real - (CONV_WIDTH - 1):t_real][None, None]

    xs = x_sample[:, 0]
    qkv_s, idx_s, z_s, xbc_s, gates_s = project(xs, bsz, "sample")
    dtx_s = rms_matmul(xs, norm_mix[l], w_dtx, tm=bsz, tn=512, exact=True, name="sample_dt")
    topk_s = min(TOPK_MAX, (past + 1) // 4)
    rows, pos = sample_select(idx_s[:, :i0].reshape(bsz, IDX_HEADS, IDX_DIM),
                              idx_s[:, i0 + IDX_DIM:i0 + IDX_DIM + IDX_HEADS].reshape(bsz, IDX_HEADS, 1),
                              idx_s[:, i0:i0 + IDX_DIM].reshape(bsz, 1, IDX_DIM),
                              cache_idx_k[l], page_table, topk=topk_s)
    k_new = qkv_s[:, att:2 * att]
    v_new = qkv_s[:, 2 * att:]
    o_att_s = sample_attention(qkv_s[:, :att].reshape(bsz, 1, att), rows.reshape(bsz, topk_s), pos,
                               k_new, v_new, rel_bias, cache_k[l].reshape(n_pool * page, att),
                               cache_v[l].reshape(n_pool * page, att), past=past)
    o_ssm_s, state_new = sample_ssm(xbc_s[:, None], state_conv[l], z_s[:, None], dtx_s[:, None],
                                    state_ssm[l].reshape(bsz, inner, SSM_STATE), *ssm_params)
    y_s = merge_ffn(xs, o_att_s.reshape(bsz, att), o_ssm_s.reshape(bsz, inner), gates_s, *out_params,
                    tm=bsz, name="sample_merge_ffn")
    y_sample = y_s[:, None]
    k_sample = k_new.reshape(1, bsz, 1, N_HEADS, HEAD_DIM)
    v_sample = v_new.reshape(1, bsz, 1, N_HEADS, HEAD_DIM)
    idxk_sample = idx_s[:, i0:i0 + IDX_DIM].reshape(1, bsz, 1, IDX_DIM)
    ssm_sample = state_new.reshape((1,) + state_ssm.shape[1:])
    conv_sample = jnp.concatenate([state_conv[l][:, 1:], xbc_s[:, None]], axis=1)[None]
    return (y_prompt, y_sample, k_prompt, v_prompt, idxk_prompt, ssm_prompt, conv_prompt,
            k_sample, v_sample, idxk_sample, ssm_sample, conv_sample)
```

```python
import functools
import math

import jax
import jax.numpy as jnp
import numpy as np
from jax import lax
from jax.experimental import pallas as pl
from jax.experimental.pallas import tpu as pltpu

F32 = jnp.float32
BF16 = jnp.bfloat16
I32 = jnp.int32

N_HEADS = 16
HEAD_DIM = 64
IDX_HEADS = 8
IDX_DIM = 64
TOPK_MAX = 256
MAX_DISTANCE = 128
SSM_HEAD_DIM = 64
SSM_GROUPS = 4
SSM_STATE = 128
CONV_WIDTH = 4
EPS = 1e-6

LANE = 128
Q_TILE = 256
ROW_PAD = 1280
VMEM_LIMIT = 60 * 1024 * 1024
HI = lax.Precision.HIGHEST
INT_MIN = -(2 ** 31)


def _cparams(sem):
    return pltpu.CompilerParams(dimension_semantics=sem, vmem_limit_bytes=VMEM_LIMIT)


def _const_spec(shape):
    nd = len(shape)
    return pl.BlockSpec(shape, lambda *a: (0,) * nd, pipeline_mode=pl.Buffered(1))


def _rms_matmul_kernel(x_ref, g_ref, w_ref, o_ref, h_scr, *, exact):
    @pl.when(pl.program_id(1) == 0)
    def _():
        x = x_ref[...]
        y = x * lax.rsqrt(jnp.mean(x * x, axis=-1, keepdims=True) + EPS) * g_ref[...]
        h_scr[...] = y.astype(h_scr.dtype)

    if exact:
        o_ref[...] = jnp.dot(h_scr[...], w_ref[...], preferred_element_type=F32, precision=HI)
    else:
        o_ref[...] = jnp.dot(h_scr[...], w_ref[...], preferred_element_type=F32).astype(o_ref.dtype)


def rms_matmul(x, g, w, *, tm, tn, exact=False, out_dtype=F32, name):
    r, d = x.shape
    n = w.shape[1]
    assert r % tm == 0 and n % tn == 0
    return pl.pallas_call(
        functools.partial(_rms_matmul_kernel, exact=exact),
        grid=(r // tm, n // tn),
        in_specs=[pl.BlockSpec((tm, d), lambda i, j: (i, 0)),
                  pl.BlockSpec((1, d), lambda i, j: (0, 0)),
                  pl.BlockSpec((d, tn), lambda i, j: (0, j))],
        out_specs=pl.BlockSpec((tm, tn), lambda i, j: (i, j)),
        out_shape=jax.ShapeDtypeStruct((r, n), out_dtype),
        scratch_shapes=[pltpu.VMEM((tm, d), F32 if exact else BF16)],
        compiler_params=_cparams(("parallel", "arbitrary")),
        name=name,
    )(x, g.reshape(1, d), w)


def _rms_qkv_kernel(x_ref, g_ref, w_ref, o_ref, t_ref, h_scr):
    j = pl.program_id(1)

    @pl.when(j == 0)
    def _():
        x = x_ref[...]
        y = x * lax.rsqrt(jnp.mean(x * x, axis=-1, keepdims=True) + EPS) * g_ref[...]
        h_scr[...] = y.astype(h_scr.dtype)

    res = jnp.dot(h_scr[...], w_ref[...], preferred_element_type=F32)
    o_ref[...] = res

    @pl.when(j > 0)
    def _():
        t_ref[...] = res.T.astype(t_ref.dtype)


def rms_qkv(x, g, w, *, tm, name):
    r, d = x.shape
    att = w.shape[1] // 3
    assert r % tm == 0
    return pl.pallas_call(
        _rms_qkv_kernel,
        grid=(r // tm, 3),
        in_specs=[pl.BlockSpec((tm, d), lambda i, j: (i, 0)),
                  pl.BlockSpec((1, d), lambda i, j: (0, 0)),
                  pl.BlockSpec((d, att), lambda i, j: (0, j))],
        out_specs=[pl.BlockSpec((tm, att), lambda i, j: (i, j)),
                   pl.BlockSpec((att, tm), lambda i, j: (jnp.maximum(j - 1, 0), i))],
        out_shape=[jax.ShapeDtypeStruct((r, 3 * att), F32),
                   jax.ShapeDtypeStruct((2 * att, r), BF16)],
        scratch_shapes=[pltpu.VMEM((tm, d), BF16)],
        compiler_params=_cparams(("parallel", "arbitrary")),
        name=name,
    )(x, g.reshape(1, d), w)


def _sortable(x):
    bits = pltpu.bitcast(x + 0.0, I32)
    return bits ^ ((bits >> 31) & 0x7FFFFFFF)


def _split_bf16(x):
    hi = x.astype(BF16)
    lo = (x - hi.astype(F32)).astype(BF16)
    return hi, lo


def _prompt_attn_kernel(q_ref, qi_ref, kw_ref, ki_ref, kT_ref, vT_ref, tab_ref, o_ref,
                        key_scr, lhs_scr, wb_scr, thr_scr, need_scr, carry_scr,
                        qh_scr, m_scr, l_scr, acc_scr, mask_scr, *, n_sub_per_tile, topk):
    bq = Q_TILE
    i = pl.program_id(0)
    j = pl.program_id(1)
    nkt = pl.num_programs(1)
    row_iota = lax.broadcasted_iota(I32, (bq, bq), 0)
    col_iota = lax.broadcasted_iota(I32, (bq, bq), 1)

    def wide(x):
        return jnp.concatenate([x] * (bq // LANE), axis=1)

    @pl.when(j == 0)
    def _select():
        qi = qi_ref[...] * (IDX_DIM ** -0.5)
        hi, lo = _split_bf16(qi)
        for h in range(IDX_HEADS):
            sl = slice(h * IDX_DIM, (h + 1) * IDX_DIM)
            lhs_scr[pl.ds(h * bq, bq), :] = jnp.concatenate(
                [hi[:, sl], lo[:, sl], hi[:, sl], lo[:, sl]], axis=1)
        w = kw_ref[...][:, IDX_DIM:IDX_DIM + IDX_HEADS] * (IDX_HEADS ** -0.5)
        for h in range(IDX_HEADS):
            wb_scr[h] = jnp.broadcast_to(w[:, h:h + 1], (bq, LANE))
        for h in range(N_HEADS):
            qh_scr[h] = (q_ref[:, h * HEAD_DIM:(h + 1) * HEAD_DIM] * (HEAD_DIM ** -0.5)).astype(BF16)
        m_scr[...] = jnp.full(m_scr.shape, -1e30, F32)
        l_scr[...] = jnp.zeros(l_scr.shape, F32)
        acc_scr[...] = jnp.zeros(acc_scr.shape, F32)
        carry_scr[...] = jnp.zeros(carry_scr.shape, F32)

        def score_body(c, _):
            khi, klo = _split_bf16(ki_ref[c])
            rhs = jnp.concatenate([khi, khi, klo, klo], axis=1)
            s = lax.dot_general(lhs_scr[...], rhs, (((1,), (1,)), ((), ())),
                                preferred_element_type=F32)
            tot = jnp.zeros((bq, bq), F32)
            for h in range(IDX_HEADS):
                tot = tot + wide(wb_scr[h]) * jnp.maximum(s[h * bq:(h + 1) * bq], 0.0)
            causal = (c * bq + col_iota) <= (i * bq + row_iota)
            tot = jnp.where(causal, tot, -jnp.inf)
            key_scr[c] = _sortable(tot)
            return 0

        lax.fori_loop(0, i + 1, score_body, 0)

        @pl.when(i % 2 == 0)
        def _():
            key_scr[i + 1] = jnp.full((bq, bq), INT_MIN, I32)

        def count_ge(cand):
            half = bq // 2
            counts = []
            for r0 in range(0, bq, half):
                cand_b = jnp.broadcast_to(cand[r0:r0 + half], (half, LANE))

                def body(c2, cnt, r0=r0, cand_b=cand_b):
                    for c in (2 * c2, 2 * c2 + 1):
                        k = key_scr[c, r0:r0 + half, :]
                        for l0 in range(0, bq, LANE):
                            cnt = cnt + jnp.where(k[:, l0:l0 + LANE] >= cand_b, 1.0, 0.0)
                    return cnt

                cnt = lax.fori_loop(0, (i + 2) // 2, body, jnp.zeros((half, LANE), F32))
                counts.append(jnp.sum(cnt, axis=1, keepdims=True))
            return jnp.concatenate(counts, axis=0)

        def bit_body(b, thr):
            cand = thr ^ jnp.left_shift(jnp.int32(1), 31 - b)
            return jnp.where(count_ge(cand) >= topk, cand, thr)

        thr = lax.fori_loop(0, 32, bit_body, jnp.full((bq, 1), INT_MIN, I32))
        n_gt = count_ge(thr + 1)
        thr_scr[...] = jnp.broadcast_to(thr, (bq, LANE))
        need_scr[...] = jnp.broadcast_to(topk - n_gt, (bq, LANE))

    nsub = n_sub_per_tile
    c0 = j * nsub

    def build_mask():
        thr = wide(thr_scr[...])
        tri = jnp.where(row_iota <= col_iota, 1.0, 0.0).astype(BF16)
        for s in range(nsub):
            c = c0 + s
            lanes = slice(s * bq, (s + 1) * bq)

            @pl.when(c <= i)
            def _(c=c, lanes=lanes):
                key = key_scr[c]
                eq = key == thr
                eqf = jnp.where(eq, 1.0, 0.0).astype(BF16)
                rank = jnp.dot(eqf, tri, preferred_element_type=F32) + wide(carry_scr[...])
                carry_scr[...] = jnp.broadcast_to(rank[:, bq - 1:bq], (bq, LANE))
                take_eq = jnp.where(eq, jnp.where(rank <= wide(need_scr[...]), 0.0, -jnp.inf), -jnp.inf)
                madd = jnp.where(key > thr, 0.0, take_eq)
                causal = (c * bq + col_iota) <= (i * bq + row_iota)
                mask_scr[:, lanes] = jnp.where(causal, madd, -jnp.inf)

            @pl.when(c > i)
            def _(lanes=lanes):
                mask_scr[:, lanes] = jnp.full((bq, bq), -jnp.inf, F32)

    def attend(near):
        def head_body(h, _):
            rows = pl.ds(pl.multiple_of(h * HEAD_DIM, HEAD_DIM), HEAD_DIM)
            logit = jnp.dot(qh_scr[h], kT_ref[rows, :], preferred_element_type=F32) + mask_scr[...]
            if near:
                parts = []
                for s in range(nsub):
                    delta = i - (c0 + s)
                    on = jnp.where((delta >= 0) & (delta <= 1), 1.0, 0.0)
                    parts.append(logit[:, s * bq:(s + 1) * bq] + on * tab_ref[jnp.clip(delta, 0, 1), h])
                logit = jnp.concatenate(parts, axis=1)
            m_old = m_scr[h]
            m_new = jnp.maximum(m_old, jnp.max(logit, axis=1, keepdims=True))
            alpha = jnp.exp(m_old - m_new)
            p = jnp.exp(logit - m_new[:, :1])
            l_scr[h] = alpha * l_scr[h] + jnp.sum(p, axis=1, keepdims=True)
            pv = lax.dot_general(p.astype(BF16), vT_ref[rows, :], (((1,), (1,)), ((), ())),
                                 preferred_element_type=F32)
            acc_scr[h] = acc_scr[h] * alpha[:, :HEAD_DIM] + pv
            m_scr[h] = m_new
            return 0

        lax.fori_loop(0, N_HEADS, head_body, 0, unroll=2)

    has_keys = c0 <= i
    is_near = c0 + nsub - 1 >= i - 1

    @pl.when(has_keys)
    def _():
        build_mask()

    @pl.when(has_keys & jnp.logical_not(is_near))
    def _():
        attend(False)

    @pl.when(has_keys & is_near)
    def _():
        attend(True)


    @pl.when(j == nkt - 1)
    def _finish():
        for h in range(N_HEADS):
            o_ref[:, h * HEAD_DIM:(h + 1) * HEAD_DIM] = (
                acc_scr[h] / l_scr[h][:, :HEAD_DIM]).astype(o_ref.dtype)


def _t5_bucket(dist, n_buckets):
    max_exact = n_buckets // 2
    d = jnp.maximum(dist, 0)
    df = jnp.maximum(d, 1).astype(F32)
    large = max_exact + (jnp.log(df / max_exact) / math.log(MAX_DISTANCE / max_exact)
                         * (n_buckets - max_exact)).astype(I32)
    return jnp.where(d < max_exact, d, jnp.minimum(large, n_buckets - 1))


def prompt_attention(q, qidx, ki, kvT, rel_bias, *, t_real):
    tp = q.shape[0]
    bq = Q_TILE
    nq = tp // bq
    kt = ROW_PAD if tp % ROW_PAD == 0 else bq
    nsub = kt // bq
    nkt = tp // kt
    topk = min(TOPK_MAX, t_real // 4)
    n_buckets = rel_bias.shape[0]
    assert 2 * bq - (bq - 1) >= MAX_DISTANCE
    qo = jnp.arange(bq, dtype=I32)[:, None]
    ko = jnp.arange(bq, dtype=I32)[None, :]
    dist = jnp.stack([qo - ko, bq + qo - ko])
    onehot = (_t5_bucket(dist, n_buckets)[None] == jnp.arange(n_buckets)[:, None, None, None]).astype(F32)
    tab = jnp.einsum("bh,bdqk->dhqk", rel_bias - rel_bias[n_buckets - 1], onehot,
                     precision=HI)

    def kv_map(part):
        return lambda i, j: (part, jnp.minimum(j, ((i + 1) * bq - 1) // kt))

    kernel = functools.partial(_prompt_attn_kernel, n_sub_per_tile=nsub, topk=topk)
    return pl.pallas_call(
        kernel,
        grid=(nq, nkt),
        in_specs=[pl.BlockSpec((bq, N_HEADS * HEAD_DIM), lambda i, j: (i, 0)),
                  pl.BlockSpec((bq, IDX_HEADS * IDX_DIM), lambda i, j: (i, 0)),
                  pl.BlockSpec((bq, LANE), lambda i, j: (i, IDX_HEADS * IDX_DIM // LANE)),
                  _const_spec((nq, bq, IDX_DIM)),
                  pl.BlockSpec((N_HEADS * HEAD_DIM, kt), kv_map(0)),
                  pl.BlockSpec((N_HEADS * HEAD_DIM, kt), kv_map(1)),
                  _const_spec((2, N_HEADS, bq, bq))],
        out_specs=pl.BlockSpec((bq, N_HEADS * HEAD_DIM), lambda i, j: (i, 0)),
        out_shape=jax.ShapeDtypeStruct((tp, N_HEADS * HEAD_DIM), BF16),
        scratch_shapes=[pltpu.VMEM((nq + 1, bq, bq), I32),
                        pltpu.VMEM((IDX_HEADS * bq, 4 * IDX_DIM), BF16),
                        pltpu.VMEM((IDX_HEADS, bq, LANE), F32),
                        pltpu.VMEM((bq, LANE), I32),
                        pltpu.VMEM((bq, LANE), F32),
                        pltpu.VMEM((bq, LANE), F32),
                        pltpu.VMEM((N_HEADS, bq, HEAD_DIM), BF16),
                        pltpu.VMEM((N_HEADS, bq, LANE), F32),
                        pltpu.VMEM((N_HEADS, bq, LANE), F32),
                        pltpu.VMEM((N_HEADS, bq, HEAD_DIM), F32),
                        pltpu.VMEM((bq, kt), F32)],
        compiler_params=_cparams(("arbitrary", "arbitrary")),
        name="prompt_attn",
    )(q, qidx, qidx, ki.reshape(nq, bq, IDX_DIM), kvT, kvT, tab)


SSD_CHUNK = 128
CONV_CARRY = 8


def _silu(x):
    return x * jax.nn.sigmoid(x)


def _softplus(x):
    return jnp.maximum(x, 0.0) + jnp.log1p(jnp.exp(-jnp.abs(x)))


def _gated_group_norm(y, xs, z, dskip, norm):
    yg = (y + dskip * xs) * _silu(z)
    gw = yg.shape[-1] // SSM_GROUPS
    outs = []
    for g in range(SSM_GROUPS):
        part = yg[:, g * gw:(g + 1) * gw]
        ms = jnp.mean(part * part, axis=-1, keepdims=True)
        outs.append(part * lax.rsqrt(ms + EPS))
    return jnp.concatenate(outs, axis=-1) * norm


def _ssd_kernel(xbc_ref, z_ref, dtr_ref, convw_ref, convb_ref, dtb_ref, alog_ref, dskip_ref, norm_ref,
                o_ref, hfin_ref, tail_scr, ht_scr, y_scr, *, t_real, n_heads, inner):
    L = SSD_CHUNK
    P = SSM_HEAD_DIM
    N = SSM_STATE
    e = n_heads // SSM_GROUPS
    ci = pl.program_id(0)

    @pl.when(ci == 0)
    def _():
        tail_scr[...] = jnp.zeros(tail_scr.shape, F32)
        ht_scr[...] = jnp.zeros(ht_scr.shape, F32)

    xbc = xbc_ref[...]
    hist = jnp.concatenate([tail_scr[...], xbc], axis=0)
    conv = convb_ref[...] + convw_ref[CONV_WIDTH - 1:CONV_WIDTH, :] * xbc
    for jx in range(CONV_WIDTH - 1):
        shifted = pltpu.roll(hist, CONV_WIDTH - 1 - jx, axis=0)[CONV_CARRY:]
        conv = conv + convw_ref[jx:jx + 1, :] * shifted
    tail_scr[...] = xbc[L - CONV_CARRY:, :]
    act = _silu(conv)
    xs = act[:, :inner]
    bm = act[:, inner:inner + SSM_GROUPS * N]
    cm = act[:, inner + SSM_GROUPS * N:]

    row = lax.broadcasted_iota(I32, (L, LANE), 0)
    dt = _softplus(dtr_ref[...] + dtb_ref[...])
    dt = jnp.where(ci * L + row < t_real, dt, 0.0)
    la = dt * (-jnp.exp(alog_ref[...]))
    r2 = lax.broadcasted_iota(I32, (L, L), 0)
    c2 = lax.broadcasted_iota(I32, (L, L), 1)
    causal = r2 >= c2
    acs = jnp.dot(jnp.where(causal, 1.0, 0.0), la, preferred_element_type=F32, precision=HI)
    acs_t = acs.T
    acs_last = acs[L - 1:L, :]
    dec_last = jnp.exp(acs_last)

    for g in range(SSM_GROUPS):
        bg = bm[:, g * N:(g + 1) * N]
        cg = cm[:, g * N:(g + 1) * N]
        cg16 = cg.astype(BF16)
        cb = lax.dot_general(cg16, bg.astype(BF16), (((1,), (1,)), ((), ())),
                             preferred_element_type=F32)
        bg_t = bg.T
        for hh in range(e):
            h = g * e + hh
            a_col = acs[:, h:h + 1]
            a_row = acs_t[h:h + 1, :]
            decay = jnp.exp(jnp.where(causal, a_col - a_row, -jnp.inf))
            xdt = (xs[:, h * P:(h + 1) * P] * dt[:, h:h + 1]).astype(BF16)
            y = jnp.dot((cb * decay).astype(BF16), xdt, preferred_element_type=F32)
            ht = ht_scr[h]
            y = y + jnp.dot(cg16, ht.astype(BF16), preferred_element_type=F32) * jnp.exp(a_col)
            y_scr[:, h * P:(h + 1) * P] = y
            tail = jnp.exp(acs_last[:, h:h + 1] - a_row)
            ht_scr[h] = ht * dec_last[:, h:h + 1] + jnp.dot(
                (bg_t * tail).astype(BF16), xdt, preferred_element_type=F32)

    o_ref[...] = _gated_group_norm(y_scr[...], xs, z_ref[...], dskip_ref[...],
                                   norm_ref[...]).astype(o_ref.dtype)

    @pl.when(ci == pl.num_programs(0) - 1)
    def _():
        hfin_ref[...] = ht_scr[...]


def _pad_lanes(v, n=LANE):
    return jnp.pad(v, (0, n - v.shape[0])).reshape(1, n)


def ssd_prompt(xbc, z, idx, conv_w, conv_b, dt_bias, a_log, d_skip, ssm_norm, *, t_real):
    tp, conv_dim = xbc.shape
    inner = z.shape[1]
    n_heads = a_log.shape[0]
    L = SSD_CHUNK
    kernel = functools.partial(_ssd_kernel, t_real=t_real, n_heads=n_heads, inner=inner)
    dtr_block = idx.shape[1] // LANE - 1
    return pl.pallas_call(
        kernel,
        grid=(tp // L,),
        in_specs=[pl.BlockSpec((L, conv_dim), lambda c: (c, 0)),
                  pl.BlockSpec((L, inner), lambda c: (c, 0)),
                  pl.BlockSpec((L, LANE), lambda c: (c, dtr_block)),
                  _const_spec((CONV_WIDTH, conv_dim)),
                  _const_spec((1, conv_dim)),
                  _const_spec((1, LANE)),
                  _const_spec((1, LANE)),
                  _const_spec((1, inner)),
                  _const_spec((1, inner))],
        out_specs=[pl.BlockSpec((L, inner), lambda c: (c, 0)),
                   pl.BlockSpec((n_heads, SSM_STATE, SSM_HEAD_DIM), lambda c: (0, 0, 0))],
        out_shape=[jax.ShapeDtypeStruct((tp, inner), BF16),
                   jax.ShapeDtypeStruct((n_heads, SSM_STATE, SSM_HEAD_DIM), F32)],
        scratch_shapes=[pltpu.VMEM((CONV_CARRY, conv_dim), F32),
                        pltpu.VMEM((n_heads, SSM_STATE, SSM_HEAD_DIM), F32),
                        pltpu.VMEM((L, inner), F32)],
        compiler_params=_cparams(("arbitrary",)),
        name="ssd_prompt",
    )(xbc, z, idx, conv_w, conv_b.reshape(1, conv_dim), _pad_lanes(dt_bias), _pad_lanes(a_log),
      jnp.repeat(d_skip, SSM_HEAD_DIM).reshape(1, inner), ssm_norm.reshape(1, inner))


def _rms(x, g):
    return x * lax.rsqrt(jnp.mean(x * x, axis=-1, keepdims=True) + EPS) * g


def _merge_ffn_kernel(x_ref, oa_ref, os_ref, gate_ref, wa_ref, ws_ref, wo_ref, nf_ref, wg_ref, wu_ref,
                      wd_ref, nfin_ref, y_ref):
    d = x_ref.shape[-1]
    dot = functools.partial(jnp.dot, preferred_element_type=F32)
    ga = jax.nn.sigmoid(gate_ref[:, :d])
    gb = jax.nn.sigmoid(gate_ref[:, d:])
    m = ga * dot(oa_ref[...].astype(BF16), wa_ref[...]) + gb * dot(os_ref[...].astype(BF16), ws_ref[...])
    x1 = x_ref[...] + dot(m.astype(BF16), wo_ref[...])
    h = _rms(x1, nf_ref[...]).astype(BF16)
    u = _silu(dot(h, wg_ref[...])) * dot(h, wu_ref[...])
    x2 = x1 + dot(u.astype(BF16), wd_ref[...])
    y_ref[...] = _rms(x2, nfin_ref[...])


def merge_ffn(x, o_att, o_ssm, gates, w_attn_out, w_ssm_out, w_out, norm_ffn, w_gate, w_up, w_down,
              norm_final, *, tm, name):
    r, d = x.shape
    row = lambda width: pl.BlockSpec((tm, width), lambda i: (i, 0))
    ws = [w_attn_out.astype(BF16), w_ssm_out.astype(BF16), w_out.astype(BF16), norm_ffn.reshape(1, d),
          w_gate.astype(BF16), w_up.astype(BF16), w_down.astype(BF16), norm_final.reshape(1, d)]
    return pl.pallas_call(
        _merge_ffn_kernel,
        grid=(r // tm,),
        in_specs=[row(d), row(o_att.shape[1]), row(o_ssm.shape[1]), row(gates.shape[1])]
        + [_const_spec(w.shape) for w in ws],
        out_specs=row(d),
        out_shape=jax.ShapeDtypeStruct((r, d), F32),
        compiler_params=_cparams(("parallel",)),
        name=name,
    )(x, o_att, o_ssm, gates, *ws)


PAGES_PER_STEP = 16


def _split3_bf16(x):
    hi = x.astype(BF16)
    rem = x - hi.astype(F32)
    mid = rem.astype(BF16)
    lo = (rem - mid.astype(F32)).astype(BF16)
    return hi, mid, lo


def _sample_scores_kernel(pt_ref, qt_ref, wt_ref, qi_ref, w_ref, knew_ref, *rest, n_pages, page):
    page_refs = rest[:PAGES_PER_STEP]
    sc_ref, snew_ref, whi_scr, wlo_scr, acc_scr = rest[PAGES_PER_STEP:]
    g = pl.program_id(1)
    dot = functools.partial(jnp.dot, preferred_element_type=F32)

    @pl.when(g == 0)
    def _():
        qt = qt_ref[...] * (IDX_DIM ** -0.5)
        owner = lax.broadcasted_iota(I32, (IDX_DIM, LANE), 1) // IDX_HEADS
        for p in range(PAGES_PER_STEP):
            hi, lo = _split_bf16(jnp.where(owner == p, qt, 0.0))
            whi_scr[p * IDX_DIM:(p + 1) * IDX_DIM, :] = hi
            wlo_scr[p * IDX_DIM:(p + 1) * IDX_DIM, :] = lo
        acc_scr[...] = jnp.zeros(acc_scr.shape, F32)

    l_hi, l_lo = _split_bf16(jnp.concatenate([r[...] for r in page_refs], axis=1))
    s = dot(l_hi, whi_scr[...]) + dot(l_lo, whi_scr[...]) + dot(l_hi, wlo_scr[...])
    r = jnp.maximum(s, 0.0) * (wt_ref[...] * (IDX_HEADS ** -0.5))
    src_page = lax.broadcasted_iota(I32, (LANE, LANE), 0) // IDX_HEADS
    gather = jnp.where(lax.broadcasted_iota(I32, (LANE, LANE), 1) == g * PAGES_PER_STEP + src_page,
                       1.0, 0.0).astype(BF16)
    r1, r2, r3 = _split3_bf16(r)
    acc_scr[...] += dot(r1, gather) + dot(r2, gather) + dot(r3, gather)

    @pl.when(g == pl.num_programs(1) - 1)
    def _():
        sc_ref[...] = acc_scr[...].T[:n_pages, :]
        qi = qi_ref[...] * (IDX_DIM ** -0.5)
        s_new = jnp.sum(qi * knew_ref[...], axis=1, keepdims=True)
        s_new = jnp.sum(w_ref[...] * (IDX_HEADS ** -0.5) * jnp.maximum(s_new, 0.0), axis=0, keepdims=True)
        snew_ref[...] = jnp.broadcast_to(s_new, snew_ref.shape)


def sample_scores(qi, w, k_new, cache_idx, page_table):
    bsz, n_pages = page_table.shape
    page = cache_idx.shape[1]
    assert page == LANE and n_pages <= LANE and n_pages % PAGES_PER_STEP == 0
    assert PAGES_PER_STEP * IDX_HEADS == LANE
    ng = n_pages // PAGES_PER_STEP
    qt = jnp.tile(jnp.swapaxes(qi, 1, 2), (1, 1, PAGES_PER_STEP))
    wt = jnp.tile(w.reshape(bsz, 1, IDX_HEADS), (1, 1, PAGES_PER_STEP))
    kernel = functools.partial(_sample_scores_kernel, n_pages=n_pages, page=page)
    per_b = lambda *shape: pl.BlockSpec((None,) + shape, lambda b, g, pt: (b, 0, 0))

    def page_spec(p):
        return pl.BlockSpec((None, page, IDX_DIM),
                            lambda b, g, pt: (pt[b, g * PAGES_PER_STEP + p], 0, 0))

    return pl.pallas_call(
        kernel,
        grid_spec=pltpu.PrefetchScalarGridSpec(
            num_scalar_prefetch=1,
            grid=(bsz, ng),
            in_specs=[per_b(IDX_DIM, LANE), per_b(1, LANE), per_b(IDX_HEADS, IDX_DIM),
                      per_b(IDX_HEADS, 1), per_b(1, IDX_DIM)]
            + [page_spec(p) for p in range(PAGES_PER_STEP)],
            out_specs=[per_b(n_pages, page), per_b(8, LANE)],
            scratch_shapes=[pltpu.VMEM((PAGES_PER_STEP * IDX_DIM, LANE), BF16),
                            pltpu.VMEM((PAGES_PER_STEP * IDX_DIM, LANE), BF16),
                            pltpu.VMEM((page, LANE), F32)]),
        out_shape=[jax.ShapeDtypeStruct((bsz, n_pages, page), F32),
                   jax.ShapeDtypeStruct((bsz, 8, LANE), F32)],
        compiler_params=_cparams(("arbitrary", "arbitrary")),
        name="sample_scores",
    )(page_table, qt, wt, qi, w, k_new, *([cache_idx] * PAGES_PER_STEP))


def _sample_pick_kernel(pt_ref, sc_ref, snew_ref, rows_ref, pos_ref, take_scr, pre_scr, nc_scr,
                        *, n_pages, page, topk):
    b = pl.program_id(0)
    bsz = sc_ref.shape[1]
    past = n_pages * page

    @pl.when(b == 0)
    def _():
        keys = _sortable(sc_ref[...])
        key_new = _sortable(snew_ref[...])

        def count_ge(cand):
            per_slot = jnp.sum(jnp.where(keys >= cand[None], 1.0, 0.0), axis=0)
            return jnp.sum(per_slot, axis=1, keepdims=True) + jnp.where(key_new >= cand, 1.0, 0.0)

        def bit_body(bit, thr):
            cand = thr ^ jnp.left_shift(jnp.int32(1), 31 - bit)
            return jnp.where(count_ge(cand) >= topk, cand, thr)

        thr = lax.fori_loop(0, 32, bit_body, jnp.full((bsz, 1), INT_MIN, I32))
        need = topk - count_ge(thr + 1)
        tri = jnp.where(lax.broadcasted_iota(I32, (page, page), 0)
                        <= lax.broadcasted_iota(I32, (page, page), 1), 1.0, 0.0).astype(BF16)

        def within_page(mask):
            flat = mask.reshape(n_pages * bsz, page).astype(BF16)
            return jnp.dot(flat, tri, preferred_element_type=F32).reshape(n_pages, bsz, page)

        eq = jnp.where(keys == thr[None], 1.0, 0.0)
        w_eq = within_page(eq)
        seen = jnp.zeros((bsz, 1), F32)
        for t in range(n_pages):
            take_eq = jnp.where(w_eq[t] + seen <= need, eq[t], 0.0)
            take_scr[t] = jnp.where(keys[t] > thr, 1.0, take_eq)
            seen = seen + w_eq[t][:, page - 1:page]
        take = take_scr[...]
        w_take = within_page(take)
        seen = jnp.zeros((bsz, 1), F32)
        for t in range(n_pages):
            pre_scr[t] = (w_take[t] + seen) * take[t]
            seen = seen + w_take[t][:, page - 1:page]
        nc_scr[...] = jnp.broadcast_to(seen, nc_scr.shape)

    slot = lax.broadcasted_iota(I32, (1, page), 1)
    n_cached = nc_scr[pl.ds(b, 1), :][:, :1]
    part = LANE
    for r0 in range(0, topk, part):
        want = lax.broadcasted_iota(I32, (part, page), 0).astype(F32) + (r0 + 1.0)

        def gather_body(t, carry, want=want):
            acc_rows, acc_pos = carry
            hit = jnp.broadcast_to(pre_scr[t, pl.ds(b, 1), :], (part, page)) == want
            phys = (pt_ref[b, t] * page + slot).astype(F32)
            logical = (t * page + slot).astype(F32)
            return (acc_rows + jnp.where(hit, phys, 0.0), acc_pos + jnp.where(hit, logical, 0.0))

        zero = jnp.zeros((part, page), F32)
        acc_rows, acc_pos = lax.fori_loop(0, n_pages, gather_body, (zero, zero))
        rows = jnp.sum(acc_rows, axis=1, keepdims=True)
        pos = jnp.sum(acc_pos, axis=1, keepdims=True)
        last = lax.broadcasted_iota(I32, (part, 1), 0).astype(F32) + r0 >= n_cached
        rows_ref[r0:r0 + part, :] = jnp.where(last, -1.0, rows).astype(I32)
        pos_ref[r0:r0 + part, :] = jnp.where(last, float(past), pos).astype(I32)


def sample_pick(scores, s_new, page_table, *, page, topk):
    n_pages, bsz, _ = scores.shape
    assert n_pages * page * page < 2 ** 24
    kernel = functools.partial(_sample_pick_kernel, n_pages=n_pages, page=page, topk=topk)
    out_spec = pl.BlockSpec((None, topk, 1), lambda b, pt: (b, 0, 0))
    return pl.pallas_call(
        kernel,
        grid_spec=pltpu.PrefetchScalarGridSpec(
            num_scalar_prefetch=1,
            grid=(bsz,),
            in_specs=[pl.BlockSpec(scores.shape, lambda b, pt: (0, 0, 0)),
                      pl.BlockSpec(s_new.shape, lambda b, pt: (0, 0))],
            out_specs=[out_spec, out_spec],
            scratch_shapes=[pltpu.VMEM((n_pages, bsz, page), F32),
                            pltpu.VMEM((n_pages, bsz, page), F32),
                            pltpu.VMEM((bsz, page), F32)]),
        out_shape=[jax.ShapeDtypeStruct((bsz, topk, 1), I32)] * 2,
        compiler_params=_cparams(("arbitrary",)),
        name="sample_pick",
    )(page_table, scores, s_new)


def _sample_attn_kernel(rows_ref, q_ref, pos_ref, knew_ref, vnew_ref, bias_ref, ck_ref, cv_ref, o_ref,
                        kbuf, vbuf, sem, *, topk, past, n_buckets):
    b = pl.program_id(0)
    nb = pl.num_programs(0)
    nh = N_HEADS

    def token_copies(src_k, src_v, token, slot, r):
        src = pl.ds(token * nh, nh)
        dst = pl.ds(r * nh, nh)
        return (pltpu.make_async_copy(src_k.at[src], kbuf.at[slot, dst], sem.at[0, slot]),
                pltpu.make_async_copy(src_v.at[src], vbuf.at[slot, dst], sem.at[1, slot]))

    def issue(seq, slot):
        def body(r, _):
            for cp in token_copies(ck_ref, cv_ref, rows_ref[seq, r], slot, r):
                cp.start()
            return 0

        lax.fori_loop(0, topk - 1, body, 0)
        last = rows_ref[seq, topk - 1]

        @pl.when(last >= 0)
        def _():
            for cp in token_copies(ck_ref, cv_ref, last, slot, topk - 1):
                cp.start()

        @pl.when(last < 0)
        def _():
            for cp in token_copies(knew_ref, vnew_ref, seq, slot, topk - 1):
                cp.start()

    def wait(slot):
        def body(r, _):
            for cp in token_copies(ck_ref, cv_ref, 0, slot, r):
                cp.wait()
            return 0

        lax.fori_loop(0, topk, body, 0)

    slot = b % 2

    @pl.when(b == 0)
    def _():
        issue(0, 0)

    @pl.when(b + 1 < nb)
    def _():
        issue(b + 1, 1 - slot)

    wait(slot)

    k3 = kbuf[slot].reshape(topk, nh, HEAD_DIM)
    v3 = vbuf[slot].reshape(topk, nh, HEAD_DIM)
    logits = jnp.sum(k3 * q_ref[...][None], axis=-1, keepdims=True) * (HEAD_DIM ** -0.5)
    bucket = _t5_bucket(past - pos_ref[...], n_buckets)
    onehot = jnp.where(bucket == lax.broadcasted_iota(I32, (topk, 1, LANE), 2), 1.0, 0.0)
    logits = logits + jnp.sum(onehot * bias_ref[...][None], axis=-1, keepdims=True)
    m = jnp.max(logits, axis=0, keepdims=True)
    p = jnp.exp(logits - m)
    p = p / jnp.sum(p, axis=0, keepdims=True)
    o_ref[...] = jnp.sum(p * v3, axis=0)


def sample_attention(q, rows, pos, k_new, v_new, rel_bias, cache_k, cache_v, *, past):
    bsz, nh, hd = q.shape
    topk = rows.shape[1]
    n_buckets = rel_bias.shape[0]
    assert n_buckets <= LANE and nh == N_HEADS and hd == HEAD_DIM
    bias_t = jnp.pad(rel_bias.T, ((0, 0), (0, LANE - n_buckets)))
    kernel = functools.partial(_sample_attn_kernel, topk=topk, past=past, n_buckets=n_buckets)
    any_spec = pl.BlockSpec(memory_space=pl.ANY)
    return pl.pallas_call(
        kernel,
        grid_spec=pltpu.PrefetchScalarGridSpec(
            num_scalar_prefetch=1,
            grid=(bsz,),
            in_specs=[pl.BlockSpec((None, nh, hd), lambda b, r: (b, 0, 0)),
                      pl.BlockSpec((None, topk, 1, 1), lambda b, r: (b, 0, 0, 0)),
                      any_spec, any_spec,
                      pl.BlockSpec(bias_t.shape, lambda b, r: (0, 0)),
                      any_spec, any_spec],
            out_specs=pl.BlockSpec((None, nh, hd), lambda b, r: (b, 0, 0)),
            scratch_shapes=[pltpu.VMEM((2, topk * nh, hd), F32),
                            pltpu.VMEM((2, topk * nh, hd), F32),
                            pltpu.SemaphoreType.DMA((2, 2))]),
        out_shape=jax.ShapeDtypeStruct((bsz, nh, hd), F32),
        compiler_params=_cparams(("arbitrary",)),
        name="sample_attn",
    )(rows, q, pos, k_new, v_new, bias_t, cache_k, cache_v)


def _sample_ssm_kernel(xbc_ref, cst_ref, z_ref, dtr_ref, h0_ref, convw_ref, convb_ref, dtb_ref, alog_ref,
                       dskip_ref, norm_ref, o_ref, h_ref, *, inner):
    N = SSM_STATE
    gw = inner // SSM_GROUPS
    conv = convb_ref[...] + convw_ref[CONV_WIDTH - 1:CONV_WIDTH, :] * xbc_ref[...]
    for jx in range(CONV_WIDTH - 1):
        conv = conv + convw_ref[jx:jx + 1, :] * cst_ref[jx:jx + 1, :]
    act = _silu(conv)
    xs = act[:, :inner]
    dt = _softplus(dtr_ref[...] + dtb_ref[...])
    decay = jnp.exp(dt * (-jnp.exp(alog_ref[...])))
    dtx = dt * xs

    def column_bcast(rowvec):
        return jnp.broadcast_to(rowvec, (LANE, inner)).T

    dec_c = column_bcast(decay)
    dtx_c = column_bcast(dtx)
    ys = []
    for g in range(SSM_GROUPS):
        rs = slice(g * gw, (g + 1) * gw)
        bg = act[:, inner + g * N:inner + (g + 1) * N]
        cg = act[:, inner + (SSM_GROUPS + g) * N:inner + (SSM_GROUPS + g + 1) * N]
        hn = h0_ref[rs, :] * dec_c[rs, :] + dtx_c[rs, :] * bg
        h_ref[rs, :] = hn
        yc = jnp.sum(hn * cg, axis=1, keepdims=True)
        ys.append(jnp.broadcast_to(yc, (gw, LANE)).T[0:1, :])
    y = jnp.concatenate(ys, axis=1)
    o_ref[...] = _gated_group_norm(y, xs, z_ref[...], dskip_ref[...], norm_ref[...])


def sample_ssm(xbc, state_conv, z, dtr_x, state, conv_w, conv_b, dt_bias, a_log, d_skip, ssm_norm):
    bsz, _, conv_dim = xbc.shape
    inner = z.shape[-1]
    rep = lambda v: jnp.repeat(v, SSM_HEAD_DIM).reshape(1, inner)
    per_b = lambda shape: pl.BlockSpec((None,) + shape, lambda b: (b, 0, 0))
    kernel = functools.partial(_sample_ssm_kernel, inner=inner)
    return pl.pallas_call(
        kernel,
        grid=(bsz,),
        in_specs=[per_b((1, conv_dim)), per_b((CONV_WIDTH - 1, conv_dim)), per_b((1, inner)),
                  per_b((1, inner)), per_b((inner, SSM_STATE)),
                  _const_spec((CONV_WIDTH, conv_dim)), _const_spec((1, conv_dim)),
                  _const_spec((1, inner)), _const_spec((1, inner)), _const_spec((1, inner)),
                  _const_spec((1, inner))],
        out_specs=[per_b((1, inner)), per_b((inner, SSM_STATE))],
        out_shape=[jax.ShapeDtypeStruct((bsz, 1, inner), F32),
                   jax.ShapeDtypeStruct((bsz, inner, SSM_STATE), F32)],
        compiler_params=_cparams(("parallel",)),
        name="sample_ssm",
    )(xbc, state_conv, z, dtr_x, state, conv_w, conv_b.reshape(1, conv_dim), rep(dt_bias),
      rep(a_log), rep(d_skip), ssm_norm.reshape(1, inner))


def _split_w_in(w_in, d_model, ssm_inner, conv_dim, ssm_heads):
    att = N_HEADS * HEAD_DIM
    sizes = (att, att, att, IDX_HEADS * IDX_DIM, IDX_DIM, IDX_HEADS, ssm_inner, conv_dim, ssm_heads,
             d_model, d_model)
    pts = [0] + [int(v) for v in np.cumsum(sizes)]
    seg = lambda a, b: w_in[:, pts[a]:pts[b]]
    w_qkv = seg(0, 3).astype(BF16)
    zeros = lambda n: jnp.zeros((w_in.shape[0], n), w_in.dtype)
    w_idx = jnp.concatenate([seg(3, 4), seg(4, 6), zeros(LANE - IDX_DIM - IDX_HEADS),
                             seg(8, 9), zeros(LANE - ssm_heads)], axis=1)
    w_z = seg(6, 7).astype(BF16)
    w_xbc = seg(7, 8).astype(BF16)
    w_gate = seg(9, 11).astype(BF16)
    w_dtx = jnp.repeat(seg(8, 9), SSM_HEAD_DIM, axis=1)
    return w_qkv, w_idx, w_z, w_xbc, w_gate, w_dtx


def kernel(x_prompt, x_sample, cache_k, cache_v, cache_idx_k, state_ssm, state_conv, page_table,
           meta_tokens, rel_bias, norm_mix, w_in, conv_w, conv_b, dt_bias, a_log, d_skip, ssm_norm,
           w_attn_out, w_ssm_out, w_out, norm_ffn, w_gate, w_up, w_down, norm_final):
    assert w_in.shape[0] == 1 and x_prompt.shape[0] == 1 and x_sample.shape[1] == 1
    d_model = x_prompt.shape[-1]
    n_meta = meta_tokens.shape[0]
    t_real = n_meta + x_prompt.shape[1]
    tp = -(-t_real // ROW_PAD) * ROW_PAD if t_real > ROW_PAD else -(-t_real // Q_TILE) * Q_TILE
    bsz = x_sample.shape[0]
    n_pool, page = cache_k.shape[1:3]
    past = page_table.shape[1] * page
    inner = ssm_norm.shape[-1]
    conv_dim = conv_w.shape[-1]
    n_ssm_heads = a_log.shape[-1]
    att = N_HEADS * HEAD_DIM
    i0 = IDX_HEADS * IDX_DIM
    l = 0
    w_qkv, w_idx, w_z, w_xbc, w_g, w_dtx = _split_w_in(w_in[l], d_model, inner, conv_dim, n_ssm_heads)
    ssm_params = (conv_w[l], conv_b[l], dt_bias[l], a_log[l], d_skip[l], ssm_norm[l])
    out_params = (w_attn_out[l], w_ssm_out[l], w_out[l], norm_ffn[l], w_gate[l], w_up[l], w_down[l],
                  norm_final)

    def project(x, tm, tag):
        proj = functools.partial(rms_matmul, x, norm_mix[l], tm=tm)
        return (proj(w_idx, tn=w_idx.shape[1], exact=True, name=tag + "_idx"),
                proj(w_z, tn=1024, name=tag + "_z"),
                proj(w_xbc, tn=1024, name=tag + "_xbc"),
                proj(w_g, tn=1024, name=tag + "_gates"))

    tm_p = 640 if tp % 640 == 0 else Q_TILE
    xp = jnp.concatenate([meta_tokens.astype(x_prompt.dtype), x_prompt[0],
                          jnp.zeros((tp - t_real, d_model), x_prompt.dtype)], axis=0)
    qkv, kv_t = rms_qkv(xp, norm_mix[l], w_qkv, tm=tm_p, name="prompt_qkv")
    idx, z, xbc, gates = project(xp, tm_p, "prompt")
    o_att = prompt_attention(qkv, idx, idx[:, i0:i0 + IDX_DIM], kv_t, rel_bias, t_real=t_real)
    o_ssm, h_t = ssd_prompt(xbc, z, idx, *ssm_params, t_real=t_real)
    y = merge_ffn(xp, o_att, o_ssm, gates, *out_params, tm=Q_TILE, name="prompt_merge_ffn")
    y_prompt = y[n_meta:t_real][None]
    k_prompt = qkv[:t_real, att:2 * att].reshape(1, 1, t_real, N_HEADS, HEAD_DIM)
    v_prompt = qkv[:t_real, 2 * att:].reshape(1, 1, t_real, N_HEADS, HEAD_DIM)
    idxk_prompt = idx[:t_real, i0:i0 + IDX_DIM][None, None]
    ssm_prompt = jnp.swapaxes(h_t, 1, 2)[None, None]
    conv_prompt = xbc[t_real - (CONV_WIDTH - 1):t_real][None, None]

    xs = x_sample[:, 0]
    qkv_s = rms_matmul(xs, norm_mix[l], w_qkv, tm=bsz, tn=1024, name="sample_qkv")
    idx_s, z_s, xbc_s, gates_s = project(xs, bsz, "sample")
    dtx_s = rms_matmul(xs, norm_mix[l], w_dtx, tm=bsz, tn=512, exact=True, name="sample_dt")
    topk_s = min(TOPK_MAX, (past + 1) // 4)
    scores, s_new = sample_scores(idx_s[:, :i0].reshape(bsz, IDX_HEADS, IDX_DIM),
                                  idx_s[:, i0 + IDX_DIM:i0 + IDX_DIM + IDX_HEADS].reshape(bsz, IDX_HEADS, 1),
                                  idx_s[:, i0:i0 + IDX_DIM].reshape(bsz, 1, IDX_DIM),
                                  cache_idx_k[l], page_table)
    rows, pos = sample_pick(jnp.swapaxes(scores, 0, 1), s_new[:, 0, :1], page_table, page=page,
                            topk=topk_s)
    k_new = qkv_s[:, att:2 * att]
    v_new = qkv_s[:, 2 * att:]
    heads = lambda a: a.reshape(-1, HEAD_DIM)
    o_att_s = sample_attention(qkv_s[:, :att].reshape(bsz, N_HEADS, HEAD_DIM), rows.reshape(bsz, topk_s),
                               pos.reshape(bsz, topk_s, 1, 1), heads(k_new), heads(v_new), rel_bias,
                               heads(cache_k[l]), heads(cache_v[l]), past=past)
    o_ssm_s, state_new = sample_ssm(xbc_s[:, None], state_conv[l], z_s[:, None], dtx_s[:, None],
                                    state_ssm[l].reshape(bsz, inner, SSM_STATE), *ssm_params)
    y_s = merge_ffn(xs, o_att_s.reshape(bsz, att), o_ssm_s.reshape(bsz, inner), gates_s, *out_params,
                    tm=bsz, name="sample_merge_ffn")
    y_sample = y_s[:, None]
    k_sample = k_new.reshape(1, bsz, 1, N_HEADS, HEAD_DIM)
    v_sample = v_new.reshape(1, bsz, 1, N_HEADS, HEAD_DIM)
    idxk_sample = idx_s[:, i0:i0 + IDX_DIM].reshape(1, bsz, 1, IDX_DIM)
    ssm_sample = state_new.reshape((1,) + state_ssm.shape[1:])
    conv_sample = jnp.concatenate([state_conv[l][:, 1:], xbc_s[:, None]], axis=1)[None]
    return (y_prompt, y_sample, k_prompt, v_prompt, idxk_prompt, ssm_prompt, conv_prompt,
            k_sample, v_sample, idxk_sample, ssm_sample, conv_sample)
```

```python
import functools
import math

import jax
import jax.numpy as jnp
import numpy as np
from jax import lax
from jax.experimental import pallas as pl
from jax.experimental.pallas import tpu as pltpu

F32 = jnp.float32
BF16 = jnp.bfloat16
I32 = jnp.int32

N_HEADS = 16
HEAD_DIM = 64
IDX_HEADS = 8
IDX_DIM = 64
TOPK_MAX = 256
MAX_DISTANCE = 128
SSM_HEAD_DIM = 64
SSM_GROUPS = 4
SSM_STATE = 128
CONV_WIDTH = 4
EPS = 1e-6

LANE = 128
Q_TILE = 256
ROW_PAD = 1280
VMEM_LIMIT = 60 * 1024 * 1024
HI = lax.Precision.HIGHEST
INT_MIN = -(2 ** 31)


def _cparams(sem):
    return pltpu.CompilerParams(dimension_semantics=sem, vmem_limit_bytes=VMEM_LIMIT)


def _const_spec(shape):
    nd = len(shape)
    return pl.BlockSpec(shape, lambda *a: (0,) * nd, pipeline_mode=pl.Buffered(1))


def _rms_matmul_kernel(x_ref, g_ref, w_ref, o_ref, h_scr, *, exact):
    @pl.when(pl.program_id(1) == 0)
    def _():
        x = x_ref[...]
        y = x * lax.rsqrt(jnp.mean(x * x, axis=-1, keepdims=True) + EPS) * g_ref[...]
        h_scr[...] = y.astype(h_scr.dtype)

    if exact:
        o_ref[...] = jnp.dot(h_scr[...], w_ref[...], preferred_element_type=F32, precision=HI)
    else:
        o_ref[...] = jnp.dot(h_scr[...], w_ref[...], preferred_element_type=F32).astype(o_ref.dtype)


def rms_matmul(x, g, w, *, tm, tn, exact=False, out_dtype=F32, name):
    r, d = x.shape
    n = w.shape[1]
    assert r % tm == 0 and n % tn == 0
    return pl.pallas_call(
        functools.partial(_rms_matmul_kernel, exact=exact),
        grid=(r // tm, n // tn),
        in_specs=[pl.BlockSpec((tm, d), lambda i, j: (i, 0)),
                  pl.BlockSpec((1, d), lambda i, j: (0, 0)),
                  pl.BlockSpec((d, tn), lambda i, j: (0, j))],
        out_specs=pl.BlockSpec((tm, tn), lambda i, j: (i, j)),
        out_shape=jax.ShapeDtypeStruct((r, n), out_dtype),
        scratch_shapes=[pltpu.VMEM((tm, d), F32 if exact else BF16)],
        compiler_params=_cparams(("parallel", "arbitrary")),
        name=name,
    )(x, g.reshape(1, d), w)


def _rms_qkv_kernel(x_ref, g_ref, w_ref, o_ref, t_ref, h_scr):
    j = pl.program_id(1)

    @pl.when(j == 0)
    def _():
        x = x_ref[...]
        y = x * lax.rsqrt(jnp.mean(x * x, axis=-1, keepdims=True) + EPS) * g_ref[...]
        h_scr[...] = y.astype(h_scr.dtype)

    res = jnp.dot(h_scr[...], w_ref[...], preferred_element_type=F32)
    o_ref[...] = res

    @pl.when(j > 0)
    def _():
        t_ref[...] = res.T.astype(t_ref.dtype)


def rms_qkv(x, g, w, *, tm, name):
    r, d = x.shape
    att = w.shape[1] // 3
    assert r % tm == 0
    return pl.pallas_call(
        _rms_qkv_kernel,
        grid=(r // tm, 3),
        in_specs=[pl.BlockSpec((tm, d), lambda i, j: (i, 0)),
                  pl.BlockSpec((1, d), lambda i, j: (0, 0)),
                  pl.BlockSpec((d, att), lambda i, j: (0, j))],
        out_specs=[pl.BlockSpec((tm, att), lambda i, j: (i, j)),
                   pl.BlockSpec((att, tm), lambda i, j: (jnp.maximum(j - 1, 0), i))],
        out_shape=[jax.ShapeDtypeStruct((r, 3 * att), F32),
                   jax.ShapeDtypeStruct((2 * att, r), BF16)],
        scratch_shapes=[pltpu.VMEM((tm, d), BF16)],
        compiler_params=_cparams(("parallel", "arbitrary")),
        name=name,
    )(x, g.reshape(1, d), w)


def _sortable(x):
    bits = pltpu.bitcast(x + 0.0, I32)
    return bits ^ ((bits >> 31) & 0x7FFFFFFF)


def _split_bf16(x):
    hi = x.astype(BF16)
    lo = (x - hi.astype(F32)).astype(BF16)
    return hi, lo


def _prompt_attn_kernel(q_ref, qi_ref, kw_ref, ki_ref, kT_ref, vT_ref, tab_ref, o_ref,
                        key_scr, lhs_scr, wb_scr, thr_scr, need_scr, carry_scr,
                        qh_scr, m_scr, l_scr, acc_scr, mask_scr, *, n_sub_per_tile, topk):
    bq = Q_TILE
    i = pl.program_id(0)
    j = pl.program_id(1)
    nkt = pl.num_programs(1)
    row_iota = lax.broadcasted_iota(I32, (bq, bq), 0)
    col_iota = lax.broadcasted_iota(I32, (bq, bq), 1)

    def wide(x):
        return jnp.concatenate([x] * (bq // LANE), axis=1)

    @pl.when(j == 0)
    def _select():
        qi = qi_ref[...] * (IDX_DIM ** -0.5)
        hi, lo = _split_bf16(qi)
        for h in range(IDX_HEADS):
            sl = slice(h * IDX_DIM, (h + 1) * IDX_DIM)
            lhs_scr[pl.ds(h * bq, bq), :] = jnp.concatenate(
                [hi[:, sl], lo[:, sl], hi[:, sl], lo[:, sl]], axis=1)
        w = kw_ref[...][:, IDX_DIM:IDX_DIM + IDX_HEADS] * (IDX_HEADS ** -0.5)
        for h in range(IDX_HEADS):
            wb_scr[h] = jnp.broadcast_to(w[:, h:h + 1], (bq, LANE))
        for h in range(N_HEADS):
            qh_scr[h] = (q_ref[:, h * HEAD_DIM:(h + 1) * HEAD_DIM] * (HEAD_DIM ** -0.5)).astype(BF16)
        m_scr[...] = jnp.full(m_scr.shape, -1e30, F32)
        l_scr[...] = jnp.zeros(l_scr.shape, F32)
        acc_scr[...] = jnp.zeros(acc_scr.shape, F32)
        carry_scr[...] = jnp.zeros(carry_scr.shape, F32)

        def score_body(c, _):
            khi, klo = _split_bf16(ki_ref[c])
            rhs = jnp.concatenate([khi, khi, klo, klo], axis=0)
            s = jnp.dot(lhs_scr[...], rhs, preferred_element_type=F32)
            tot = jnp.zeros((bq, bq), F32)
            for h in range(IDX_HEADS):
                tot = tot + wide(wb_scr[h]) * jnp.maximum(s[h * bq:(h + 1) * bq], 0.0)
            causal = (c * bq + col_iota) <= (i * bq + row_iota)
            tot = jnp.where(causal, tot, -jnp.inf)
            key_scr[c] = _sortable(tot)
            return 0

        lax.fori_loop(0, i + 1, score_body, 0)

        @pl.when(i % 2 == 0)
        def _():
            key_scr[i + 1] = jnp.full((bq, bq), INT_MIN, I32)

        def count_ge(cand):
            half = bq // 2
            counts = []
            for r0 in range(0, bq, half):
                cand_b = jnp.broadcast_to(cand[r0:r0 + half], (half, LANE))

                def body(c2, cnt, r0=r0, cand_b=cand_b):
                    for c in (2 * c2, 2 * c2 + 1):
                        k = key_scr[c, r0:r0 + half, :]
                        for l0 in range(0, bq, LANE):
                            cnt = cnt + jnp.where(k[:, l0:l0 + LANE] >= cand_b, 1.0, 0.0)
                    return cnt

                cnt = lax.fori_loop(0, (i + 2) // 2, body, jnp.zeros((half, LANE), F32))
                counts.append(jnp.sum(cnt, axis=1, keepdims=True))
            return jnp.concatenate(counts, axis=0)

        def bit_body(b, thr):
            cand = thr ^ jnp.left_shift(jnp.int32(1), 31 - b)
            return jnp.where(count_ge(cand) >= topk, cand, thr)

        thr = lax.fori_loop(0, 32, bit_body, jnp.full((bq, 1), INT_MIN, I32))
        n_gt = count_ge(thr + 1)
        thr_scr[...] = jnp.broadcast_to(thr, (bq, LANE))
        need_scr[...] = jnp.broadcast_to(topk - n_gt, (bq, LANE))

    nsub = n_sub_per_tile
    c0 = j * nsub

    def build_mask():
        thr = wide(thr_scr[...])
        tri = jnp.where(row_iota <= col_iota, 1.0, 0.0).astype(BF16)
        for s in range(nsub):
            c = c0 + s
            lanes = slice(s * bq, (s + 1) * bq)

            @pl.when(c <= i)
            def _(c=c, lanes=lanes):
                key = key_scr[c]
                eq = key == thr
                eqf = jnp.where(eq, 1.0, 0.0).astype(BF16)
                rank = jnp.dot(eqf, tri, preferred_element_type=F32) + wide(carry_scr[...])
                carry_scr[...] = jnp.broadcast_to(rank[:, bq - 1:bq], (bq, LANE))
                take_eq = jnp.where(eq, jnp.where(rank <= wide(need_scr[...]), 0.0, -jnp.inf), -jnp.inf)
                madd = jnp.where(key > thr, 0.0, take_eq)
                causal = (c * bq + col_iota) <= (i * bq + row_iota)
                mask_scr[:, lanes] = jnp.where(causal, madd, -jnp.inf)

            @pl.when(c > i)
            def _(lanes=lanes):
                mask_scr[:, lanes] = jnp.full((bq, bq), -jnp.inf, F32)

    def attend(near):
        def head_body(h, _):
            rows = pl.ds(pl.multiple_of(h * HEAD_DIM, HEAD_DIM), HEAD_DIM)
            logit = jnp.dot(qh_scr[h], kT_ref[rows, :], preferred_element_type=F32) + mask_scr[...]
            if near:
                parts = []
                for s in range(nsub):
                    delta = i - (c0 + s)
                    on = jnp.where((delta >= 0) & (delta <= 1), 1.0, 0.0)
                    parts.append(logit[:, s * bq:(s + 1) * bq] + on * tab_ref[jnp.clip(delta, 0, 1), h])
                logit = jnp.concatenate(parts, axis=1)
            m_old = m_scr[h]
            m_new = jnp.maximum(m_old, jnp.max(logit, axis=1, keepdims=True))
            alpha = jnp.exp(m_old - m_new)
            p = jnp.exp(logit - m_new[:, :1])
            l_scr[h] = alpha * l_scr[h] + jnp.sum(p, axis=1, keepdims=True)
            pv = lax.dot_general(p.astype(BF16), vT_ref[rows, :], (((1,), (1,)), ((), ())),
                                 preferred_element_type=F32)
            acc_scr[h] = acc_scr[h] * alpha[:, :HEAD_DIM] + pv
            m_scr[h] = m_new
            return 0

        lax.fori_loop(0, N_HEADS, head_body, 0, unroll=4)

    has_keys = c0 <= i
    is_near = c0 + nsub - 1 >= i - 1

    @pl.when(has_keys)
    def _():
        build_mask()

    @pl.when(has_keys & jnp.logical_not(is_near))
    def _():
        attend(False)

    @pl.when(has_keys & is_near)
    def _():
        attend(True)


    @pl.when(j == nkt - 1)
    def _finish():
        for h in range(N_HEADS):
            o_ref[:, h * HEAD_DIM:(h + 1) * HEAD_DIM] = (
                acc_scr[h] / l_scr[h][:, :HEAD_DIM]).astype(o_ref.dtype)


def _t5_bucket(dist, n_buckets):
    max_exact = n_buckets // 2
    d = jnp.maximum(dist, 0)
    df = jnp.maximum(d, 1).astype(F32)
    large = max_exact + (jnp.log(df / max_exact) / math.log(MAX_DISTANCE / max_exact)
                         * (n_buckets - max_exact)).astype(I32)
    return jnp.where(d < max_exact, d, jnp.minimum(large, n_buckets - 1))


def prompt_attention(q, qidx, ki, kvT, rel_bias, *, t_real):
    tp = q.shape[0]
    bq = Q_TILE
    nq = tp // bq
    kt = ROW_PAD if tp % ROW_PAD == 0 else bq
    nsub = kt // bq
    nkt = tp // kt
    topk = min(TOPK_MAX, t_real // 4)
    n_buckets = rel_bias.shape[0]
    assert 2 * bq - (bq - 1) >= MAX_DISTANCE
    qo = jnp.arange(bq, dtype=I32)[:, None]
    ko = jnp.arange(bq, dtype=I32)[None, :]
    dist = jnp.stack([qo - ko, bq + qo - ko])
    onehot = (_t5_bucket(dist, n_buckets)[None] == jnp.arange(n_buckets)[:, None, None, None]).astype(F32)
    tab = jnp.einsum("bh,bdqk->dhqk", rel_bias - rel_bias[n_buckets - 1], onehot,
                     precision=HI)

    def kv_map(part):
        return lambda i, j: (part, jnp.minimum(j, ((i + 1) * bq - 1) // kt))

    kernel = functools.partial(_prompt_attn_kernel, n_sub_per_tile=nsub, topk=topk)
    return pl.pallas_call(
        kernel,
        grid=(nq, nkt),
        in_specs=[pl.BlockSpec((bq, N_HEADS * HEAD_DIM), lambda i, j: (i, 0)),
                  pl.BlockSpec((bq, IDX_HEADS * IDX_DIM), lambda i, j: (i, 0)),
                  pl.BlockSpec((bq, LANE), lambda i, j: (i, IDX_HEADS * IDX_DIM // LANE)),
                  _const_spec((nq, IDX_DIM, bq)),
                  pl.BlockSpec((N_HEADS * HEAD_DIM, kt), kv_map(0)),
                  pl.BlockSpec((N_HEADS * HEAD_DIM, kt), kv_map(1)),
                  _const_spec((2, N_HEADS, bq, bq))],
        out_specs=pl.BlockSpec((bq, N_HEADS * HEAD_DIM), lambda i, j: (i, 0)),
        out_shape=jax.ShapeDtypeStruct((tp, N_HEADS * HEAD_DIM), BF16),
        scratch_shapes=[pltpu.VMEM((nq + 1, bq, bq), I32),
                        pltpu.VMEM((IDX_HEADS * bq, 4 * IDX_DIM), BF16),
                        pltpu.VMEM((IDX_HEADS, bq, LANE), F32),
                        pltpu.VMEM((bq, LANE), I32),
                        pltpu.VMEM((bq, LANE), F32),
                        pltpu.VMEM((bq, LANE), F32),
                        pltpu.VMEM((N_HEADS, bq, HEAD_DIM), BF16),
                        pltpu.VMEM((N_HEADS, bq, LANE), F32),
                        pltpu.VMEM((N_HEADS, bq, LANE), F32),
                        pltpu.VMEM((N_HEADS, bq, HEAD_DIM), F32),
                        pltpu.VMEM((bq, kt), F32)],
        compiler_params=_cparams(("arbitrary", "arbitrary")),
        name="prompt_attn",
    )(q, qidx, qidx, jnp.swapaxes(ki.reshape(nq, bq, IDX_DIM), 1, 2), kvT, kvT, tab)


SSD_CHUNK = 128
CONV_CARRY = 8


def _silu(x):
    return x * jax.nn.sigmoid(x)


def _softplus(x):
    return jnp.maximum(x, 0.0) + jnp.log1p(jnp.exp(-jnp.abs(x)))


def _gated_group_norm(y, xs, z, dskip, norm):
    yg = (y + dskip * xs) * _silu(z)
    gw = yg.shape[-1] // SSM_GROUPS
    outs = []
    for g in range(SSM_GROUPS):
        part = yg[:, g * gw:(g + 1) * gw]
        ms = jnp.mean(part * part, axis=-1, keepdims=True)
        outs.append(part * lax.rsqrt(ms + EPS))
    return jnp.concatenate(outs, axis=-1) * norm


def _ssd_kernel(xbc_ref, z_ref, dtr_ref, convw_ref, convb_ref, dtb_ref, alog_ref, dskip_ref, norm_ref,
                o_ref, hfin_ref, tail_scr, ht_scr, y_scr, *, t_real, n_heads, inner):
    L = SSD_CHUNK
    P = SSM_HEAD_DIM
    N = SSM_STATE
    e = n_heads // SSM_GROUPS
    ci = pl.program_id(0)

    @pl.when(ci == 0)
    def _():
        tail_scr[...] = jnp.zeros(tail_scr.shape, F32)
        ht_scr[...] = jnp.zeros(ht_scr.shape, F32)

    xbc = xbc_ref[...]
    hist = jnp.concatenate([tail_scr[...], xbc], axis=0)
    conv = convb_ref[...] + convw_ref[CONV_WIDTH - 1:CONV_WIDTH, :] * xbc
    for jx in range(CONV_WIDTH - 1):
        shifted = pltpu.roll(hist, CONV_WIDTH - 1 - jx, axis=0)[CONV_CARRY:]
        conv = conv + convw_ref[jx:jx + 1, :] * shifted
    tail_scr[...] = xbc[L - CONV_CARRY:, :]
    act = _silu(conv)
    xs = act[:, :inner]
    bm = act[:, inner:inner + SSM_GROUPS * N]
    cm = act[:, inner + SSM_GROUPS * N:]

    row = lax.broadcasted_iota(I32, (L, LANE), 0)
    dt = _softplus(dtr_ref[...] + dtb_ref[...])
    dt = jnp.where(ci * L + row < t_real, dt, 0.0)
    la = dt * (-jnp.exp(alog_ref[...]))
    r2 = lax.broadcasted_iota(I32, (L, L), 0)
    c2 = lax.broadcasted_iota(I32, (L, L), 1)
    causal = r2 >= c2
    acs = jnp.dot(jnp.where(causal, 1.0, 0.0), la, preferred_element_type=F32, precision=HI)
    acs_t = acs.T
    acs_last = acs[L - 1:L, :]
    dec_last = jnp.exp(acs_last)

    for g in range(SSM_GROUPS):
        bg = bm[:, g * N:(g + 1) * N]
        cg = cm[:, g * N:(g + 1) * N]
        cg16 = cg.astype(BF16)
        cb = lax.dot_general(cg16, bg.astype(BF16), (((1,), (1,)), ((), ())),
                             preferred_element_type=F32)
        bg_t = bg.T
        for hh in range(e):
            h = g * e + hh
            a_col = acs[:, h:h + 1]
            a_row = acs_t[h:h + 1, :]
            decay = jnp.exp(jnp.where(causal, a_col - a_row, -jnp.inf))
            xdt = (xs[:, h * P:(h + 1) * P] * dt[:, h:h + 1]).astype(BF16)
            y = jnp.dot((cb * decay).astype(BF16), xdt, preferred_element_type=F32)
            ht = ht_scr[h]
            y = y + jnp.dot(cg16, ht.astype(BF16), preferred_element_type=F32) * jnp.exp(a_col)
            y_scr[:, h * P:(h + 1) * P] = y
            tail = jnp.exp(acs_last[:, h:h + 1] - a_row)
            ht_scr[h] = ht * dec_last[:, h:h + 1] + jnp.dot(
                (bg_t * tail).astype(BF16), xdt, preferred_element_type=F32)

    o_ref[...] = _gated_group_norm(y_scr[...], xs, z_ref[...], dskip_ref[...],
                                   norm_ref[...]).astype(o_ref.dtype)

    @pl.when(ci == pl.num_programs(0) - 1)
    def _():
        hfin_ref[...] = ht_scr[...]


def _pad_lanes(v, n=LANE):
    return jnp.pad(v, (0, n - v.shape[0])).reshape(1, n)


def ssd_prompt(xbc, z, idx, conv_w, conv_b, dt_bias, a_log, d_skip, ssm_norm, *, t_real):
    tp, conv_dim = xbc.shape
    inner = z.shape[1]
    n_heads = a_log.shape[0]
    L = SSD_CHUNK
    kernel = functools.partial(_ssd_kernel, t_real=t_real, n_heads=n_heads, inner=inner)
    dtr_block = idx.shape[1] // LANE - 1
    return pl.pallas_call(
        kernel,
        grid=(tp // L,),
        in_specs=[pl.BlockSpec((L, conv_dim), lambda c: (c, 0)),
                  pl.BlockSpec((L, inner), lambda c: (c, 0)),
                  pl.BlockSpec((L, LANE), lambda c: (c, dtr_block)),
                  _const_spec((CONV_WIDTH, conv_dim)),
                  _const_spec((1, conv_dim)),
                  _const_spec((1, LANE)),
                  _const_spec((1, LANE)),
                  _const_spec((1, inner)),
                  _const_spec((1, inner))],
        out_specs=[pl.BlockSpec((L, inner), lambda c: (c, 0)),
                   pl.BlockSpec((n_heads, SSM_STATE, SSM_HEAD_DIM), lambda c: (0, 0, 0))],
        out_shape=[jax.ShapeDtypeStruct((tp, inner), BF16),
                   jax.ShapeDtypeStruct((n_heads, SSM_STATE, SSM_HEAD_DIM), F32)],
        scratch_shapes=[pltpu.VMEM((CONV_CARRY, conv_dim), F32),
                        pltpu.VMEM((n_heads, SSM_STATE, SSM_HEAD_DIM), F32),
                        pltpu.VMEM((L, inner), F32)],
        compiler_params=_cparams(("arbitrary",)),
        name="ssd_prompt",
    )(xbc, z, idx, conv_w, conv_b.reshape(1, conv_dim), _pad_lanes(dt_bias), _pad_lanes(a_log),
      jnp.repeat(d_skip, SSM_HEAD_DIM).reshape(1, inner), ssm_norm.reshape(1, inner))


def _rms(x, g):
    return x * lax.rsqrt(jnp.mean(x * x, axis=-1, keepdims=True) + EPS) * g


def _merge_ffn_kernel(x_ref, oa_ref, os_ref, gate_ref, wa_ref, ws_ref, wo_ref, nf_ref, wg_ref, wu_ref,
                      wd_ref, nfin_ref, y_ref):
    d = x_ref.shape[-1]
    dot = functools.partial(jnp.dot, preferred_element_type=F32)
    ga = jax.nn.sigmoid(gate_ref[:, :d])
    gb = jax.nn.sigmoid(gate_ref[:, d:])
    m = ga * dot(oa_ref[...].astype(BF16), wa_ref[...]) + gb * dot(os_ref[...].astype(BF16), ws_ref[...])
    x1 = x_ref[...] + dot(m.astype(BF16), wo_ref[...])
    h = _rms(x1, nf_ref[...]).astype(BF16)
    u = _silu(dot(h, wg_ref[...])) * dot(h, wu_ref[...])
    x2 = x1 + dot(u.astype(BF16), wd_ref[...])
    y_ref[...] = _rms(x2, nfin_ref[...])


def merge_ffn(x, o_att, o_ssm, gates, w_attn_out, w_ssm_out, w_out, norm_ffn, w_gate, w_up, w_down,
              norm_final, *, tm, name):
    r, d = x.shape
    row = lambda width: pl.BlockSpec((tm, width), lambda i: (i, 0))
    ws = [w_attn_out.astype(BF16), w_ssm_out.astype(BF16), w_out.astype(BF16), norm_ffn.reshape(1, d),
          w_gate.astype(BF16), w_up.astype(BF16), w_down.astype(BF16), norm_final.reshape(1, d)]
    return pl.pallas_call(
        _merge_ffn_kernel,
        grid=(r // tm,),
        in_specs=[row(d), row(o_att.shape[1]), row(o_ssm.shape[1]), row(gates.shape[1])]
        + [_const_spec(w.shape) for w in ws],
        out_specs=row(d),
        out_shape=jax.ShapeDtypeStruct((r, d), F32),
        compiler_params=_cparams(("parallel",)),
        name=name,
    )(x, o_att, o_ssm, gates, *ws)


PAGES_PER_STEP = 16
KV_PAGES_PER_STEP = 8


def _sample_scores_kernel(pt_ref, qc_ref, w_ref, knc_ref, *rest):
    page_refs = rest[:PAGES_PER_STEP]
    sc_ref, snew_ref, qb_scr = rest[PAGES_PER_STEP:]
    g = pl.program_id(1)
    w = w_ref[...] * (IDX_HEADS ** -0.5)

    def score(dots):
        return jnp.sum(w * jnp.maximum(dots, 0.0), axis=0)

    @pl.when(g == 0)
    def _():
        qc = qc_ref[...] * (IDX_DIM ** -0.5)
        qb_scr[...] = jnp.broadcast_to(qc, qb_scr.shape)
        s_new = score(jnp.sum(qc * knc_ref[...][None], axis=1, keepdims=True))
        snew_ref[...] = jnp.broadcast_to(s_new, snew_ref.shape)

    for p in range(PAGES_PER_STEP):
        dots = jnp.sum(qb_scr[...] * page_refs[p][...][None], axis=1, keepdims=True)
        sc_ref[pl.ds(g * PAGES_PER_STEP + p, 1), :] = score(dots)


def sample_scores(qi, w, k_new, cache_idx_t, page_table, *, layer):
    bsz, n_pages = page_table.shape
    page = cache_idx_t.shape[-1]
    assert n_pages % PAGES_PER_STEP == 0
    ng = n_pages // PAGES_PER_STEP
    per_b = lambda *shape: pl.BlockSpec((None,) + shape, lambda b, g, pt: (b,) + (0,) * len(shape))

    def page_spec(p):
        return pl.BlockSpec((None, None, IDX_DIM, page),
                            lambda b, g, pt: (layer, pt[b, g * PAGES_PER_STEP + p], 0, 0))

    return pl.pallas_call(
        _sample_scores_kernel,
        grid_spec=pltpu.PrefetchScalarGridSpec(
            num_scalar_prefetch=1,
            grid=(bsz, ng),
            in_specs=[per_b(IDX_HEADS, IDX_DIM, 1), per_b(IDX_HEADS, 1, 1), per_b(IDX_DIM, 1)]
            + [page_spec(p) for p in range(PAGES_PER_STEP)],
            out_specs=[per_b(n_pages, page), per_b(8, LANE)],
            scratch_shapes=[pltpu.VMEM((IDX_HEADS, IDX_DIM, page), F32)]),
        out_shape=[jax.ShapeDtypeStruct((bsz, n_pages, page), F32),
                   jax.ShapeDtypeStruct((bsz, 8, LANE), F32)],
        compiler_params=_cparams(("arbitrary", "arbitrary")),
        name="sample_scores",
    )(page_table, qi[..., None], w[..., None, None], k_new[..., None],
      *([cache_idx_t] * PAGES_PER_STEP))


def _sample_pick_kernel(sc_ref, snew_ref, take_ref, newsel_ref, *, topk):
    n_pages, bsz, page = sc_ref.shape
    keys = _sortable(sc_ref[...])
    key_new = _sortable(snew_ref[...])

    def count_ge(cand):
        per_slot = jnp.sum(jnp.where(keys >= cand[None], 1.0, 0.0), axis=0)
        return jnp.sum(per_slot, axis=1, keepdims=True) + jnp.where(key_new >= cand, 1.0, 0.0)

    def bit_body(bit, thr):
        cand = thr ^ jnp.left_shift(jnp.int32(1), 31 - bit)
        return jnp.where(count_ge(cand) >= topk, cand, thr)

    thr = lax.fori_loop(0, 32, bit_body, jnp.full((bsz, 1), INT_MIN, I32))
    need = topk - count_ge(thr + 1)
    tri = jnp.where(lax.broadcasted_iota(I32, (page, page), 0)
                    <= lax.broadcasted_iota(I32, (page, page), 1), 1.0, 0.0).astype(BF16)
    eq = jnp.where(keys == thr[None], 1.0, 0.0)
    w_eq = jnp.dot(eq.reshape(n_pages * bsz, page).astype(BF16), tri,
                   preferred_element_type=F32).reshape(n_pages, bsz, page)
    seen = jnp.zeros((bsz, 1), F32)
    n_taken = jnp.zeros((bsz, page), F32)
    for t in range(n_pages):
        take_eq = jnp.where(w_eq[t] + seen <= need, eq[t], 0.0)
        take = jnp.where(keys[t] > thr, 1.0, take_eq)
        take_ref[t] = take
        n_taken = n_taken + take
        seen = seen + w_eq[t][:, page - 1:page]
    n_cached = jnp.sum(n_taken, axis=1, keepdims=True)
    newsel_ref[...] = jnp.broadcast_to(topk - n_cached, newsel_ref.shape)


def sample_pick(scores, s_new, *, topk):
    n_pages, bsz, page = scores.shape
    full = lambda shape: pl.BlockSpec(shape, lambda i: (0,) * len(shape))
    return pl.pallas_call(
        functools.partial(_sample_pick_kernel, topk=topk),
        grid=(1,),
        in_specs=[full(scores.shape), full(s_new.shape)],
        out_specs=[full(scores.shape), full((bsz, LANE))],
        out_shape=[jax.ShapeDtypeStruct(scores.shape, F32), jax.ShapeDtypeStruct((bsz, LANE), F32)],
        compiler_params=_cparams(("arbitrary",)),
        name="sample_pick",
    )(scores, s_new)


def _sample_attn_kernel(pt_ref, qc_ref, knc_ref, vnc_ref, take_ref, newsel_ref, near_ref, newb_ref,
                        *rest):
    npg = KV_PAGES_PER_STEP
    k_refs, v_refs = rest[:npg], rest[npg:2 * npg]
    o_ref, qb_scr, m_scr, l_scr, acc_scr = rest[2 * npg:]
    g = pl.program_id(1)
    ng = pl.num_programs(1)
    n_pages = take_ref.shape[0]

    @pl.when(g == 0)
    def _():
        qb_scr[...] = jnp.broadcast_to(qc_ref[...] * (HEAD_DIM ** -0.5), qb_scr.shape)
        m_scr[...] = jnp.full(m_scr.shape, -1e30, F32)
        l_scr[...] = jnp.zeros(l_scr.shape, F32)
        acc_scr[...] = jnp.zeros(acc_scr.shape, F32)

    logits = []
    for p in range(npg):
        t = g * npg + p
        s = jnp.sum(qb_scr[...] * k_refs[p][...], axis=1, keepdims=True)
        on_last = jnp.where(t == n_pages - 1, 1.0, 0.0)
        s = s + on_last * near_ref[...]
        logits.append(jnp.where(take_ref[pl.ds(t, 1), :][None] > 0.5, s, -jnp.inf))
    m_old = m_scr[...]
    tile_max = functools.reduce(jnp.maximum, logits)
    m_new = jnp.maximum(m_old, jnp.max(tile_max, axis=2, keepdims=True))
    alpha = jnp.exp(m_old - m_new)
    acc = acc_scr[...] * alpha
    psum = jnp.zeros_like(tile_max)
    for p in range(npg):
        pr = jnp.exp(logits[p] - m_new)
        psum = psum + pr
        acc = acc + v_refs[p][...] * pr
    acc_scr[...] = acc
    l_new = alpha * l_scr[...] + jnp.sum(psum, axis=2, keepdims=True)
    l_scr[...] = l_new
    m_scr[...] = m_new

    @pl.when(g == ng - 1)
    def _():
        q1 = qb_scr[...][:, :, :1]
        s_new = jnp.sum(q1 * knc_ref[...], axis=1, keepdims=True) + newb_ref[...][:, :, :1]
        s_new = jnp.where(newsel_ref[...][:, :1][None] > 0.5, s_new, -jnp.inf)
        m_cache = m_new[:, :, :1]
        m_fin = jnp.maximum(m_cache, s_new)
        a = jnp.exp(m_cache - m_fin)
        pn = jnp.exp(s_new - m_fin)
        ctx = jnp.sum(acc, axis=2, keepdims=True)
        o_ref[...] = (a * ctx + pn * vnc_ref[...]) / (a * l_new[:, :, :1] + pn)


def sample_attention(q, k_new, v_new, take, newsel, rel_bias, cache_k_t, cache_v_t, page_table,
                     *, layer):
    bsz, nh, hd = q.shape
    n_pages = page_table.shape[1]
    page = cache_k_t.shape[-1]
    npg = KV_PAGES_PER_STEP
    n_buckets = rel_bias.shape[0]
    assert page == LANE and n_pages % npg == 0 and page + 1 >= MAX_DISTANCE
    dist = jnp.concatenate([page - jnp.arange(page, dtype=I32), jnp.zeros((1,), I32)])
    onehot = (_t5_bucket(dist, n_buckets)[None] == jnp.arange(n_buckets)[:, None]).astype(F32)
    rel = jnp.einsum("bh,bs->hs", rel_bias - rel_bias[n_buckets - 1], onehot, precision=HI)
    near = rel[:, None, :page]
    newb = jnp.broadcast_to(rel[:, None, page:], (nh, 1, LANE))
    per_b = lambda *shape: pl.BlockSpec((None,) + shape, lambda b, g, pt: (b,) + (0,) * len(shape))
    const = lambda *shape: pl.BlockSpec(shape, lambda b, g, pt: (0,) * len(shape))

    def page_spec(p):
        return pl.BlockSpec((None, None, nh, hd, page),
                            lambda b, g, pt: (layer, pt[b, g * npg + p], 0, 0, 0))

    return pl.pallas_call(
        _sample_attn_kernel,
        grid_spec=pltpu.PrefetchScalarGridSpec(
            num_scalar_prefetch=1,
            grid=(bsz, n_pages // npg),
            in_specs=[per_b(nh, hd, 1), per_b(nh, hd, 1), per_b(nh, hd, 1), per_b(n_pages, page),
                      per_b(1, LANE), const(nh, 1, page), const(nh, 1, LANE)]
            + [page_spec(p) for p in range(npg)] * 2,
            out_specs=per_b(nh, hd, 1),
            scratch_shapes=[pltpu.VMEM((nh, hd, page), F32),
                            pltpu.VMEM((nh, 1, LANE), F32),
                            pltpu.VMEM((nh, 1, LANE), F32),
                            pltpu.VMEM((nh, hd, page), F32)]),
        out_shape=jax.ShapeDtypeStruct((bsz, nh, hd, 1), F32),
        compiler_params=_cparams(("arbitrary", "arbitrary")),
        name="sample_attn",
    )(page_table, q[..., None], k_new[..., None], v_new[..., None], take, newsel[:, None, :],
      near, newb, *([cache_k_t] * npg), *([cache_v_t] * npg))


def _sample_ssm_kernel(xbc_ref, cst_ref, z_ref, dtr_ref, h0_ref, convw_ref, convb_ref, dtb_ref, alog_ref,
                       dskip_ref, norm_ref, o_ref, h_ref, *, inner):
    N = SSM_STATE
    gw = inner // SSM_GROUPS
    conv = convb_ref[...] + convw_ref[CONV_WIDTH - 1:CONV_WIDTH, :] * xbc_ref[...]
    for jx in range(CONV_WIDTH - 1):
        conv = conv + convw_ref[jx:jx + 1, :] * cst_ref[jx:jx + 1, :]
    act = _silu(conv)
    xs = act[:, :inner]
    dt = _softplus(dtr_ref[...] + dtb_ref[...])
    decay = jnp.exp(dt * (-jnp.exp(alog_ref[...])))
    dtx = dt * xs

    def column_bcast(rowvec):
        return jnp.broadcast_to(rowvec, (LANE, inner)).T

    dec_c = column_bcast(decay)
    dtx_c = column_bcast(dtx)
    ys = []
    for g in range(SSM_GROUPS):
        rs = slice(g * gw, (g + 1) * gw)
        bg = act[:, inner + g * N:inner + (g + 1) * N]
        cg = act[:, inner + (SSM_GROUPS + g) * N:inner + (SSM_GROUPS + g + 1) * N]
        hn = h0_ref[rs, :] * dec_c[rs, :] + dtx_c[rs, :] * bg
        h_ref[rs, :] = hn
        yc = jnp.sum(hn * cg, axis=1, keepdims=True)
        ys.append(jnp.broadcast_to(yc, (gw, LANE)).T[0:1, :])
    y = jnp.concatenate(ys, axis=1)
    o_ref[...] = _gated_group_norm(y, xs, z_ref[...], dskip_ref[...], norm_ref[...])


def sample_ssm(xbc, state_conv, z, dtr_x, state, conv_w, conv_b, dt_bias, a_log, d_skip, ssm_norm):
    bsz, _, conv_dim = xbc.shape
    inner = z.shape[-1]
    rep = lambda v: jnp.repeat(v, SSM_HEAD_DIM).reshape(1, inner)
    per_b = lambda shape: pl.BlockSpec((None,) + shape, lambda b: (b, 0, 0))
    kernel = functools.partial(_sample_ssm_kernel, inner=inner)
    return pl.pallas_call(
        kernel,
        grid=(bsz,),
        in_specs=[per_b((1, conv_dim)), per_b((CONV_WIDTH - 1, conv_dim)), per_b((1, inner)),
                  per_b((1, inner)), per_b((inner, SSM_STATE)),
                  _const_spec((CONV_WIDTH, conv_dim)), _const_spec((1, conv_dim)),
                  _const_spec((1, inner)), _const_spec((1, inner)), _const_spec((1, inner)),
                  _const_spec((1, inner))],
        out_specs=[per_b((1, inner)), per_b((inner, SSM_STATE))],
        out_shape=[jax.ShapeDtypeStruct((bsz, 1, inner), F32),
                   jax.ShapeDtypeStruct((bsz, inner, SSM_STATE), F32)],
        compiler_params=_cparams(("parallel",)),
        name="sample_ssm",
    )(xbc, state_conv, z, dtr_x, state, conv_w, conv_b.reshape(1, conv_dim), rep(dt_bias),
      rep(a_log), rep(d_skip), ssm_norm.reshape(1, inner))


def _split_w_in(w_in, d_model, ssm_inner, conv_dim, ssm_heads):
    att = N_HEADS * HEAD_DIM
    sizes = (att, att, att, IDX_HEADS * IDX_DIM, IDX_DIM, IDX_HEADS, ssm_inner, conv_dim, ssm_heads,
             d_model, d_model)
    pts = [0] + [int(v) for v in np.cumsum(sizes)]
    seg = lambda a, b: w_in[:, pts[a]:pts[b]]
    w_qkv = seg(0, 3).astype(BF16)
    zeros = lambda n: jnp.zeros((w_in.shape[0], n), w_in.dtype)
    w_idx = jnp.concatenate([seg(3, 4), seg(4, 6), zeros(LANE - IDX_DIM - IDX_HEADS),
                             seg(8, 9), zeros(LANE - ssm_heads)], axis=1)
    w_z = seg(6, 7).astype(BF16)
    w_xbc = seg(7, 8).astype(BF16)
    w_gate = seg(9, 11).astype(BF16)
    w_dtx = jnp.repeat(seg(8, 9), SSM_HEAD_DIM, axis=1)
    return w_qkv, w_idx, w_z, w_xbc, w_gate, w_dtx


def kernel(x_prompt, x_sample, cache_k, cache_v, cache_idx_k, state_ssm, state_conv, page_table,
           meta_tokens, rel_bias, norm_mix, w_in, conv_w, conv_b, dt_bias, a_log, d_skip, ssm_norm,
           w_attn_out, w_ssm_out, w_out, norm_ffn, w_gate, w_up, w_down, norm_final):
    assert w_in.shape[0] == 1 and x_prompt.shape[0] == 1 and x_sample.shape[1] == 1
    d_model = x_prompt.shape[-1]
    n_meta = meta_tokens.shape[0]
    t_real = n_meta + x_prompt.shape[1]
    tp = -(-t_real // ROW_PAD) * ROW_PAD if t_real > ROW_PAD else -(-t_real // Q_TILE) * Q_TILE
    bsz = x_sample.shape[0]
    n_pool, page = cache_k.shape[1:3]
    past = page_table.shape[1] * page
    inner = ssm_norm.shape[-1]
    conv_dim = conv_w.shape[-1]
    n_ssm_heads = a_log.shape[-1]
    att = N_HEADS * HEAD_DIM
    i0 = IDX_HEADS * IDX_DIM
    l = 0
    w_qkv, w_idx, w_z, w_xbc, w_g, w_dtx = _split_w_in(w_in[l], d_model, inner, conv_dim, n_ssm_heads)
    ssm_params = (conv_w[l], conv_b[l], dt_bias[l], a_log[l], d_skip[l], ssm_norm[l])
    out_params = (w_attn_out[l], w_ssm_out[l], w_out[l], norm_ffn[l], w_gate[l], w_up[l], w_down[l],
                  norm_final)

    def project(x, tm, tag):
        proj = functools.partial(rms_matmul, x, norm_mix[l], tm=tm)
        return (proj(w_idx, tn=w_idx.shape[1], exact=True, name=tag + "_idx"),
                proj(w_z, tn=1024, name=tag + "_z"),
                proj(w_xbc, tn=1024, name=tag + "_xbc"),
                proj(w_g, tn=1024, name=tag + "_gates"))

    tm_p = 640 if tp % 640 == 0 else Q_TILE
    xp = jnp.concatenate([meta_tokens.astype(x_prompt.dtype), x_prompt[0],
                          jnp.zeros((tp - t_real, d_model), x_prompt.dtype)], axis=0)
    qkv, kv_t = rms_qkv(xp, norm_mix[l], w_qkv, tm=tm_p, name="prompt_qkv")
    idx, z, xbc, gates = project(xp, tm_p, "prompt")
    o_att = prompt_attention(qkv, idx, idx[:, i0:i0 + IDX_DIM], kv_t, rel_bias, t_real=t_real)
    o_ssm, h_t = ssd_prompt(xbc, z, idx, *ssm_params, t_real=t_real)
    y = merge_ffn(xp, o_att, o_ssm, gates, *out_params, tm=Q_TILE, name="prompt_merge_ffn")
    y_prompt = y[n_meta:t_real][None]
    k_prompt = qkv[:t_real, att:2 * att].reshape(1, 1, t_real, N_HEADS, HEAD_DIM)
    v_prompt = qkv[:t_real, 2 * att:].reshape(1, 1, t_real, N_HEADS, HEAD_DIM)
    idxk_prompt = idx[:t_real, i0:i0 + IDX_DIM][None, None]
    ssm_prompt = jnp.swapaxes(h_t, 1, 2)[None, None]
    conv_prompt = xbc[t_real - (CONV_WIDTH - 1):t_real][None, None]

    xs = x_sample[:, 0]
    qkv_s = rms_matmul(xs, norm_mix[l], w_qkv, tm=bsz, tn=1024, name="sample_qkv")
    idx_s, z_s, xbc_s, gates_s = project(xs, bsz, "sample")
    dtx_s = rms_matmul(xs, norm_mix[l], w_dtx, tm=bsz, tn=512, exact=True, name="sample_dt")
    topk_s = min(TOPK_MAX, (past + 1) // 4)
    cache_idx_t = jnp.swapaxes(cache_idx_k, 2, 3)
    cache_k_t = jnp.transpose(cache_k, (0, 1, 3, 4, 2))
    cache_v_t = jnp.transpose(cache_v, (0, 1, 3, 4, 2))
    scores, s_new = sample_scores(idx_s[:, :i0].reshape(bsz, IDX_HEADS, IDX_DIM),
                                  idx_s[:, i0 + IDX_DIM:i0 + IDX_DIM + IDX_HEADS],
                                  idx_s[:, i0:i0 + IDX_DIM], cache_idx_t, page_table, layer=l)
    take, newsel = sample_pick(jnp.swapaxes(scores, 0, 1), s_new[:, 0, :1], topk=topk_s)
    k_new = qkv_s[:, att:2 * att]
    v_new = qkv_s[:, 2 * att:]
    heads = lambda a: a.reshape(bsz, N_HEADS, HEAD_DIM)
    o_att_s = sample_attention(heads(qkv_s[:, :att]), heads(k_new), heads(v_new),
                               jnp.swapaxes(take, 0, 1), newsel, rel_bias, cache_k_t, cache_v_t,
                               page_table, layer=l)
    o_ssm_s, state_new = sample_ssm(xbc_s[:, None], state_conv[l], z_s[:, None], dtx_s[:, None],
                                    state_ssm[l].reshape(bsz, inner, SSM_STATE), *ssm_params)
    y_s = merge_ffn(xs, o_att_s.reshape(bsz, att), o_ssm_s.reshape(bsz, inner), gates_s, *out_params,
                    tm=bsz, name="sample_merge_ffn")
    y_sample = y_s[:, None]
    k_sample = k_new.reshape(1, bsz, 1, N_HEADS, HEAD_DIM)
    v_sample = v_new.reshape(1, bsz, 1, N_HEADS, HEAD_DIM)
    idxk_sample = idx_s[:, i0:i0 + IDX_DIM].reshape(1, bsz, 1, IDX_DIM)
    ssm_sample = state_new.reshape((1,) + state_ssm.shape[1:])
    conv_sample = jnp.concatenate([state_conv[l][:, 1:], xbc_s[:, None]], axis=1)[None]
    return (y_prompt, y_sample, k_prompt, v_prompt, idxk_prompt, ssm_prompt, conv_prompt,
            k_sample, v_sample, idxk_sample, ssm_sample, conv_sample)
```

```python
import functools
import math

import jax
import jax.numpy as jnp
import numpy as np
from jax import lax
from jax.experimental import pallas as pl
from jax.experimental.pallas import tpu as pltpu

F32 = jnp.float32
BF16 = jnp.bfloat16
I32 = jnp.int32

N_HEADS = 16
HEAD_DIM = 64
IDX_HEADS = 8
IDX_DIM = 64
TOPK_MAX = 256
MAX_DISTANCE = 128
SSM_HEAD_DIM = 64
SSM_GROUPS = 4
SSM_STATE = 128
CONV_WIDTH = 4
EPS = 1e-6

LANE = 128
Q_TILE = 256
ROW_PAD = 1280
VMEM_LIMIT = 60 * 1024 * 1024
HI = lax.Precision.HIGHEST
INT_MIN = -(2 ** 31)


def _cparams(sem):
    return pltpu.CompilerParams(dimension_semantics=sem, vmem_limit_bytes=VMEM_LIMIT)


def _const_spec(shape):
    nd = len(shape)
    return pl.BlockSpec(shape, lambda *a: (0,) * nd, pipeline_mode=pl.Buffered(1))


def _rms_matmul_kernel(x_ref, g_ref, w_ref, o_ref, h_scr, *, exact):
    @pl.when(pl.program_id(1) == 0)
    def _():
        x = x_ref[...]
        y = x * lax.rsqrt(jnp.mean(x * x, axis=-1, keepdims=True) + EPS) * g_ref[...]
        h_scr[...] = y.astype(h_scr.dtype)

    if exact:
        o_ref[...] = jnp.dot(h_scr[...], w_ref[...], preferred_element_type=F32, precision=HI)
    else:
        o_ref[...] = jnp.dot(h_scr[...], w_ref[...], preferred_element_type=F32).astype(o_ref.dtype)


def rms_matmul(x, g, w, *, tm, tn, exact=False, out_dtype=F32, name):
    r, d = x.shape
    n = w.shape[1]
    assert r % tm == 0 and n % tn == 0
    return pl.pallas_call(
        functools.partial(_rms_matmul_kernel, exact=exact),
        grid=(r // tm, n // tn),
        in_specs=[pl.BlockSpec((tm, d), lambda i, j: (i, 0)),
                  pl.BlockSpec((1, d), lambda i, j: (0, 0)),
                  pl.BlockSpec((d, tn), lambda i, j: (0, j))],
        out_specs=pl.BlockSpec((tm, tn), lambda i, j: (i, j)),
        out_shape=jax.ShapeDtypeStruct((r, n), out_dtype),
        scratch_shapes=[pltpu.VMEM((tm, d), F32 if exact else BF16)],
        compiler_params=_cparams(("parallel", "arbitrary")),
        name=name,
    )(x, g.reshape(1, d), w)


def _rms_qkv_kernel(x_ref, g_ref, w_ref, o_ref, t_ref, h_scr):
    j = pl.program_id(1)

    @pl.when(j == 0)
    def _():
        x = x_ref[...]
        y = x * lax.rsqrt(jnp.mean(x * x, axis=-1, keepdims=True) + EPS) * g_ref[...]
        h_scr[...] = y.astype(h_scr.dtype)

    res = jnp.dot(h_scr[...], w_ref[...], preferred_element_type=F32)
    o_ref[...] = res

    @pl.when(j > 0)
    def _():
        t_ref[...] = res.T.astype(t_ref.dtype)


def rms_qkv(x, g, w, *, tm, name):
    r, d = x.shape
    att = w.shape[1] // 3
    assert r % tm == 0
    return pl.pallas_call(
        _rms_qkv_kernel,
        grid=(r // tm, 3),
        in_specs=[pl.BlockSpec((tm, d), lambda i, j: (i, 0)),
                  pl.BlockSpec((1, d), lambda i, j: (0, 0)),
                  pl.BlockSpec((d, att), lambda i, j: (0, j))],
        out_specs=[pl.BlockSpec((tm, att), lambda i, j: (i, j)),
                   pl.BlockSpec((att, tm), lambda i, j: (jnp.maximum(j - 1, 0), i))],
        out_shape=[jax.ShapeDtypeStruct((r, 3 * att), F32),
                   jax.ShapeDtypeStruct((2 * att, r), BF16)],
        scratch_shapes=[pltpu.VMEM((tm, d), BF16)],
        compiler_params=_cparams(("parallel", "arbitrary")),
        name=name,
    )(x, g.reshape(1, d), w)


def _sortable(x):
    bits = pltpu.bitcast(x + 0.0, I32)
    return bits ^ ((bits >> 31) & 0x7FFFFFFF)


def _split_bf16(x):
    hi = x.astype(BF16)
    lo = (x - hi.astype(F32)).astype(BF16)
    return hi, lo


def _prompt_attn_kernel(q_ref, qi_ref, kw_ref, ki_ref, kT_ref, vT_ref, tab_ref, o_ref,
                        key_scr, lhs_scr, wb_scr, thr_scr, need_scr, carry_scr,
                        qh_scr, m_scr, l_scr, acc_scr, mask_scr, *, n_sub_per_tile, topk):
    bq = Q_TILE
    i = pl.program_id(0)
    j = pl.program_id(1)
    nkt = pl.num_programs(1)
    row_iota = lax.broadcasted_iota(I32, (bq, bq), 0)
    col_iota = lax.broadcasted_iota(I32, (bq, bq), 1)

    def wide(x):
        return jnp.concatenate([x] * (bq // LANE), axis=1)

    @pl.when(j == 0)
    def _select():
        qi = qi_ref[...] * (IDX_DIM ** -0.5)
        hi, lo = _split_bf16(qi)
        for h in range(IDX_HEADS):
            sl = slice(h * IDX_DIM, (h + 1) * IDX_DIM)
            lhs_scr[pl.ds(h * bq, bq), :] = jnp.concatenate(
                [hi[:, sl], lo[:, sl], hi[:, sl], lo[:, sl]], axis=1)
        w = kw_ref[...][:, IDX_DIM:IDX_DIM + IDX_HEADS] * (IDX_HEADS ** -0.5)
        for h in range(IDX_HEADS):
            wb_scr[h] = jnp.broadcast_to(w[:, h:h + 1], (bq, LANE))
        for h in range(N_HEADS):
            qh_scr[h] = (q_ref[:, h * HEAD_DIM:(h + 1) * HEAD_DIM] * (HEAD_DIM ** -0.5)).astype(BF16)
        m_scr[...] = jnp.full(m_scr.shape, -1e30, F32)
        l_scr[...] = jnp.zeros(l_scr.shape, F32)
        acc_scr[...] = jnp.zeros(acc_scr.shape, F32)
        carry_scr[...] = jnp.zeros(carry_scr.shape, F32)

        def score_body(c, _):
            khi, klo = _split_bf16(ki_ref[c])
            rhs = jnp.concatenate([khi, khi, klo, klo], axis=0)
            s = jnp.dot(lhs_scr[...], rhs, preferred_element_type=F32)
            tot = jnp.zeros((bq, bq), F32)
            for h in range(IDX_HEADS):
                tot = tot + wide(wb_scr[h]) * jnp.maximum(s[h * bq:(h + 1) * bq], 0.0)
            causal = (c * bq + col_iota) <= (i * bq + row_iota)
            tot = jnp.where(causal, tot, -jnp.inf)
            key_scr[c] = _sortable(tot)
            return 0

        lax.fori_loop(0, i + 1, score_body, 0)

        @pl.when(i % 2 == 0)
        def _():
            key_scr[i + 1] = jnp.full((bq, bq), INT_MIN, I32)

        def count_ge(cand):
            half = bq // 2
            counts = []
            for r0 in range(0, bq, half):
                cand_b = jnp.broadcast_to(cand[r0:r0 + half], (half, LANE))

                def body(c2, cnt, r0=r0, cand_b=cand_b):
                    for c in (2 * c2, 2 * c2 + 1):
                        k = key_scr[c, r0:r0 + half, :]
                        for l0 in range(0, bq, LANE):
                            cnt = cnt + jnp.where(k[:, l0:l0 + LANE] >= cand_b, 1.0, 0.0)
                    return cnt

                cnt = lax.fori_loop(0, (i + 2) // 2, body, jnp.zeros((half, LANE), F32))
                counts.append(jnp.sum(cnt, axis=1, keepdims=True))
            return jnp.concatenate(counts, axis=0)

        def bit_body(b, thr):
            cand = thr ^ jnp.left_shift(jnp.int32(1), 31 - b)
            return jnp.where(count_ge(cand) >= topk, cand, thr)

        thr = lax.fori_loop(0, 32, bit_body, jnp.full((bq, 1), INT_MIN, I32))
        n_gt = count_ge(thr + 1)
        thr_scr[...] = jnp.broadcast_to(thr, (bq, LANE))
        need_scr[...] = jnp.broadcast_to(topk - n_gt, (bq, LANE))

    nsub = n_sub_per_tile
    c0 = j * nsub

    def build_mask():
        thr = wide(thr_scr[...])
        tri = jnp.where(row_iota <= col_iota, 1.0, 0.0).astype(BF16)
        for s in range(nsub):
            c = c0 + s
            lanes = slice(s * bq, (s + 1) * bq)

            @pl.when(c <= i)
            def _(c=c, lanes=lanes):
                key = key_scr[c]
                eq = key == thr
                eqf = jnp.where(eq, 1.0, 0.0).astype(BF16)
                rank = jnp.dot(eqf, tri, preferred_element_type=F32) + wide(carry_scr[...])
                carry_scr[...] = jnp.broadcast_to(rank[:, bq - 1:bq], (bq, LANE))
                take_eq = jnp.where(eq, jnp.where(rank <= wide(need_scr[...]), 0.0, -jnp.inf), -jnp.inf)
                madd = jnp.where(key > thr, 0.0, take_eq)
                causal = (c * bq + col_iota) <= (i * bq + row_iota)
                mask_scr[:, lanes] = jnp.where(causal, madd, -jnp.inf)

            @pl.when(c > i)
            def _(lanes=lanes):
                mask_scr[:, lanes] = jnp.full((bq, bq), -jnp.inf, F32)

    def attend(near):
        group = 4

        def group_body(hg, _):
            heads = [hg * group + k for k in range(group)]
            rows = [pl.ds(pl.multiple_of(h * HEAD_DIM, HEAD_DIM), HEAD_DIM) for h in heads]
            logits = []
            for h, r in zip(heads, rows):
                logit = jnp.dot(qh_scr[h], kT_ref[r, :], preferred_element_type=F32) + mask_scr[...]
                if near:
                    parts = []
                    for s in range(nsub):
                        delta = i - (c0 + s)
                        on = jnp.where((delta >= 0) & (delta <= 1), 1.0, 0.0)
                        parts.append(logit[:, s * bq:(s + 1) * bq]
                                     + on * tab_ref[jnp.clip(delta, 0, 1), h])
                    logit = jnp.concatenate(parts, axis=1)
                logits.append(logit)
            probs, alphas = [], []
            for h, logit in zip(heads, logits):
                m_old = m_scr[h]
                m_new = jnp.maximum(m_old, jnp.max(logit, axis=1, keepdims=True))
                alpha = jnp.exp(m_old - m_new)
                p = jnp.exp(logit - m_new[:, :1])
                l_scr[h] = alpha * l_scr[h] + jnp.sum(p, axis=1, keepdims=True)
                m_scr[h] = m_new
                probs.append(p.astype(BF16))
                alphas.append(alpha)
            for h, r, p, alpha in zip(heads, rows, probs, alphas):
                pv = lax.dot_general(p, vT_ref[r, :], (((1,), (1,)), ((), ())),
                                     preferred_element_type=F32)
                acc_scr[h] = acc_scr[h] * alpha[:, :HEAD_DIM] + pv
            return 0

        lax.fori_loop(0, N_HEADS // group, group_body, 0)

    has_keys = c0 <= i
    is_near = c0 + nsub - 1 >= i - 1

    @pl.when(has_keys)
    def _():
        build_mask()

    @pl.when(has_keys & jnp.logical_not(is_near))
    def _():
        attend(False)

    @pl.when(has_keys & is_near)
    def _():
        attend(True)


    @pl.when(j == nkt - 1)
    def _finish():
        for h in range(N_HEADS):
            o_ref[:, h * HEAD_DIM:(h + 1) * HEAD_DIM] = (
                acc_scr[h] / l_scr[h][:, :HEAD_DIM]).astype(o_ref.dtype)


def _t5_bucket(dist, n_buckets):
    max_exact = n_buckets // 2
    d = jnp.maximum(dist, 0)
    df = jnp.maximum(d, 1).astype(F32)
    large = max_exact + (jnp.log(df / max_exact) / math.log(MAX_DISTANCE / max_exact)
                         * (n_buckets - max_exact)).astype(I32)
    return jnp.where(d < max_exact, d, jnp.minimum(large, n_buckets - 1))


def prompt_attention(q, qidx, ki, kvT, rel_bias, *, t_real):
    tp = q.shape[0]
    bq = Q_TILE
    nq = tp // bq
    kt = ROW_PAD if tp % ROW_PAD == 0 else bq
    nsub = kt // bq
    nkt = tp // kt
    topk = min(TOPK_MAX, t_real // 4)
    n_buckets = rel_bias.shape[0]
    assert 2 * bq - (bq - 1) >= MAX_DISTANCE
    qo = jnp.arange(bq, dtype=I32)[:, None]
    ko = jnp.arange(bq, dtype=I32)[None, :]
    dist = jnp.stack([qo - ko, bq + qo - ko])
    onehot = (_t5_bucket(dist, n_buckets)[None] == jnp.arange(n_buckets)[:, None, None, None]).astype(F32)
    tab = jnp.einsum("bh,bdqk->dhqk", rel_bias - rel_bias[n_buckets - 1], onehot,
                     precision=HI)

    def kv_map(part):
        return lambda i, j: (part, jnp.minimum(j, ((i + 1) * bq - 1) // kt))

    kernel = functools.partial(_prompt_attn_kernel, n_sub_per_tile=nsub, topk=topk)
    return pl.pallas_call(
        kernel,
        grid=(nq, nkt),
        in_specs=[pl.BlockSpec((bq, N_HEADS * HEAD_DIM), lambda i, j: (i, 0)),
                  pl.BlockSpec((bq, IDX_HEADS * IDX_DIM), lambda i, j: (i, 0)),
                  pl.BlockSpec((bq, LANE), lambda i, j: (i, IDX_HEADS * IDX_DIM // LANE)),
                  _const_spec((nq, IDX_DIM, bq)),
                  pl.BlockSpec((N_HEADS * HEAD_DIM, kt), kv_map(0)),
                  pl.BlockSpec((N_HEADS * HEAD_DIM, kt), kv_map(1)),
                  _const_spec((2, N_HEADS, bq, bq))],
        out_specs=pl.BlockSpec((bq, N_HEADS * HEAD_DIM), lambda i, j: (i, 0)),
        out_shape=jax.ShapeDtypeStruct((tp, N_HEADS * HEAD_DIM), BF16),
        scratch_shapes=[pltpu.VMEM((nq + 1, bq, bq), I32),
                        pltpu.VMEM((IDX_HEADS * bq, 4 * IDX_DIM), BF16),
                        pltpu.VMEM((IDX_HEADS, bq, LANE), F32),
                        pltpu.VMEM((bq, LANE), I32),
                        pltpu.VMEM((bq, LANE), F32),
                        pltpu.VMEM((bq, LANE), F32),
                        pltpu.VMEM((N_HEADS, bq, HEAD_DIM), BF16),
                        pltpu.VMEM((N_HEADS, bq, LANE), F32),
                        pltpu.VMEM((N_HEADS, bq, LANE), F32),
                        pltpu.VMEM((N_HEADS, bq, HEAD_DIM), F32),
                        pltpu.VMEM((bq, kt), F32)],
        compiler_params=_cparams(("arbitrary", "arbitrary")),
        name="prompt_attn",
    )(q, qidx, qidx, jnp.swapaxes(ki.reshape(nq, bq, IDX_DIM), 1, 2), kvT, kvT, tab)


SSD_CHUNK = 128
CONV_CARRY = 8


def _silu(x):
    return x * jax.nn.sigmoid(x)


def _softplus(x):
    return jnp.maximum(x, 0.0) + jnp.log1p(jnp.exp(-jnp.abs(x)))


def _gated_group_norm(y, xs, z, dskip, norm):
    yg = (y + dskip * xs) * _silu(z)
    gw = yg.shape[-1] // SSM_GROUPS
    outs = []
    for g in range(SSM_GROUPS):
        part = yg[:, g * gw:(g + 1) * gw]
        ms = jnp.mean(part * part, axis=-1, keepdims=True)
        outs.append(part * lax.rsqrt(ms + EPS))
    return jnp.concatenate(outs, axis=-1) * norm


def _ssd_kernel(xbc_ref, z_ref, dtr_ref, convw_ref, convb_ref, dtb_ref, alog_ref, dskip_ref, norm_ref,
                o_ref, hfin_ref, tail_scr, ht_scr, y_scr, *, t_real, n_heads, inner):
    L = SSD_CHUNK
    P = SSM_HEAD_DIM
    N = SSM_STATE
    e = n_heads // SSM_GROUPS
    ci = pl.program_id(0)

    @pl.when(ci == 0)
    def _():
        tail_scr[...] = jnp.zeros(tail_scr.shape, F32)
        ht_scr[...] = jnp.zeros(ht_scr.shape, F32)

    xbc = xbc_ref[...]
    hist = jnp.concatenate([tail_scr[...], xbc], axis=0)
    conv = convb_ref[...] + convw_ref[CONV_WIDTH - 1:CONV_WIDTH, :] * xbc
    for jx in range(CONV_WIDTH - 1):
        shifted = pltpu.roll(hist, CONV_WIDTH - 1 - jx, axis=0)[CONV_CARRY:]
        conv = conv + convw_ref[jx:jx + 1, :] * shifted
    tail_scr[...] = xbc[L - CONV_CARRY:, :]
    act = _silu(conv)
    xs = act[:, :inner]
    bm = act[:, inner:inner + SSM_GROUPS * N]
    cm = act[:, inner + SSM_GROUPS * N:]

    row = lax.broadcasted_iota(I32, (L, LANE), 0)
    dt = _softplus(dtr_ref[...] + dtb_ref[...])
    dt = jnp.where(ci * L + row < t_real, dt, 0.0)
    la = dt * (-jnp.exp(alog_ref[...]))
    r2 = lax.broadcasted_iota(I32, (L, L), 0)
    c2 = lax.broadcasted_iota(I32, (L, L), 1)
    causal = r2 >= c2
    acs = jnp.dot(jnp.where(causal, 1.0, 0.0), la, preferred_element_type=F32, precision=HI)
    acs_t = acs.T
    acs_last = acs[L - 1:L, :]
    dec_last = jnp.exp(acs_last)

    for g in range(SSM_GROUPS):
        bg = bm[:, g * N:(g + 1) * N]
        cg = cm[:, g * N:(g + 1) * N]
        cg16 = cg.astype(BF16)
        cb = lax.dot_general(cg16, bg.astype(BF16), (((1,), (1,)), ((), ())),
                             preferred_element_type=F32)
        bg_t = bg.T
        for hh in range(e):
            h = g * e + hh
            a_col = acs[:, h:h + 1]
            a_row = acs_t[h:h + 1, :]
            decay = jnp.exp(jnp.where(causal, a_col - a_row, -jnp.inf))
            xdt = (xs[:, h * P:(h + 1) * P] * dt[:, h:h + 1]).astype(BF16)
            y = jnp.dot((cb * decay).astype(BF16), xdt, preferred_element_type=F32)
            ht = ht_scr[h]
            y = y + jnp.dot(cg16, ht.astype(BF16), preferred_element_type=F32) * jnp.exp(a_col)
            y_scr[:, h * P:(h + 1) * P] = y
            tail = jnp.exp(acs_last[:, h:h + 1] - a_row)
            ht_scr[h] = ht * dec_last[:, h:h + 1] + jnp.dot(
                (bg_t * tail).astype(BF16), xdt, preferred_element_type=F32)

    o_ref[...] = _gated_group_norm(y_scr[...], xs, z_ref[...], dskip_ref[...],
                                   norm_ref[...]).astype(o_ref.dtype)

    @pl.when(ci == pl.num_programs(0) - 1)
    def _():
        hfin_ref[...] = ht_scr[...]


def _pad_lanes(v, n=LANE):
    return jnp.pad(v, (0, n - v.shape[0])).reshape(1, n)


def ssd_prompt(xbc, z, idx, conv_w, conv_b, dt_bias, a_log, d_skip, ssm_norm, *, t_real):
    tp, conv_dim = xbc.shape
    inner = z.shape[1]
    n_heads = a_log.shape[0]
    L = SSD_CHUNK
    kernel = functools.partial(_ssd_kernel, t_real=t_real, n_heads=n_heads, inner=inner)
    dtr_block = idx.shape[1] // LANE - 1
    return pl.pallas_call(
        kernel,
        grid=(tp // L,),
        in_specs=[pl.BlockSpec((L, conv_dim), lambda c: (c, 0)),
                  pl.BlockSpec((L, inner), lambda c: (c, 0)),
                  pl.BlockSpec((L, LANE), lambda c: (c, dtr_block)),
                  _const_spec((CONV_WIDTH, conv_dim)),
                  _const_spec((1, conv_dim)),
                  _const_spec((1, LANE)),
                  _const_spec((1, LANE)),
                  _const_spec((1, inner)),
                  _const_spec((1, inner))],
        out_specs=[pl.BlockSpec((L, inner), lambda c: (c, 0)),
                   pl.BlockSpec((n_heads, SSM_STATE, SSM_HEAD_DIM), lambda c: (0, 0, 0))],
        out_shape=[jax.ShapeDtypeStruct((tp, inner), BF16),
                   jax.ShapeDtypeStruct((n_heads, SSM_STATE, SSM_HEAD_DIM), F32)],
        scratch_shapes=[pltpu.VMEM((CONV_CARRY, conv_dim), F32),
                        pltpu.VMEM((n_heads, SSM_STATE, SSM_HEAD_DIM), F32),
                        pltpu.VMEM((L, inner), F32)],
        compiler_params=_cparams(("arbitrary",)),
        name="ssd_prompt",
    )(xbc, z, idx, conv_w, conv_b.reshape(1, conv_dim), _pad_lanes(dt_bias), _pad_lanes(a_log),
      jnp.repeat(d_skip, SSM_HEAD_DIM).reshape(1, inner), ssm_norm.reshape(1, inner))


def _rms(x, g):
    return x * lax.rsqrt(jnp.mean(x * x, axis=-1, keepdims=True) + EPS) * g


def _merge_ffn_kernel(x_ref, oa_ref, os_ref, gate_ref, wa_ref, ws_ref, wo_ref, nf_ref, wg_ref, wu_ref,
                      wd_ref, nfin_ref, y_ref):
    d = x_ref.shape[-1]
    dot = functools.partial(jnp.dot, preferred_element_type=F32)
    ga = jax.nn.sigmoid(gate_ref[:, :d])
    gb = jax.nn.sigmoid(gate_ref[:, d:])
    m = ga * dot(oa_ref[...].astype(BF16), wa_ref[...]) + gb * dot(os_ref[...].astype(BF16), ws_ref[...])
    x1 = x_ref[...] + dot(m.astype(BF16), wo_ref[...])
    h = _rms(x1, nf_ref[...]).astype(BF16)
    u = _silu(dot(h, wg_ref[...])) * dot(h, wu_ref[...])
    x2 = x1 + dot(u.astype(BF16), wd_ref[...])
    y_ref[...] = _rms(x2, nfin_ref[...])


def merge_ffn(x, o_att, o_ssm, gates, w_attn_out, w_ssm_out, w_out, norm_ffn, w_gate, w_up, w_down,
              norm_final, *, tm, name):
    r, d = x.shape
    row = lambda width: pl.BlockSpec((tm, width), lambda i: (i, 0))
    ws = [w_attn_out.astype(BF16), w_ssm_out.astype(BF16), w_out.astype(BF16), norm_ffn.reshape(1, d),
          w_gate.astype(BF16), w_up.astype(BF16), w_down.astype(BF16), norm_final.reshape(1, d)]
    return pl.pallas_call(
        _merge_ffn_kernel,
        grid=(r // tm,),
        in_specs=[row(d), row(o_att.shape[1]), row(o_ssm.shape[1]), row(gates.shape[1])]
        + [_const_spec(w.shape) for w in ws],
        out_specs=row(d),
        out_shape=jax.ShapeDtypeStruct((r, d), F32),
        compiler_params=_cparams(("parallel",)),
        name=name,
    )(x, o_att, o_ssm, gates, *ws)


PAGES_PER_STEP = 16
KV_PAGES_PER_STEP = 8


def _sample_scores_kernel(pt_ref, qc_ref, w_ref, knc_ref, *rest):
    page_refs = rest[:PAGES_PER_STEP]
    sc_ref, snew_ref, qb_scr = rest[PAGES_PER_STEP:]
    g = pl.program_id(1)
    w = w_ref[...] * (IDX_HEADS ** -0.5)

    def score(dots):
        return jnp.sum(w * jnp.maximum(dots, 0.0), axis=0)

    @pl.when(g == 0)
    def _():
        qc = qc_ref[...] * (IDX_DIM ** -0.5)
        qb_scr[...] = jnp.broadcast_to(qc, qb_scr.shape)
        s_new = score(jnp.sum(qc * knc_ref[...][None], axis=1, keepdims=True))
        snew_ref[...] = jnp.broadcast_to(s_new, snew_ref.shape)

    for p in range(PAGES_PER_STEP):
        dots = jnp.sum(qb_scr[...] * page_refs[p][...][None], axis=1, keepdims=True)
        sc_ref[pl.ds(g * PAGES_PER_STEP + p, 1), :] = score(dots)


def sample_scores(qi, w, k_new, cache_idx_t, page_table, *, layer):
    bsz, n_pages = page_table.shape
    page = cache_idx_t.shape[-1]
    assert n_pages % PAGES_PER_STEP == 0
    ng = n_pages // PAGES_PER_STEP
    per_b = lambda *shape: pl.BlockSpec((None,) + shape, lambda b, g, pt: (b,) + (0,) * len(shape))

    def page_spec(p):
        return pl.BlockSpec((None, None, IDX_DIM, page),
                            lambda b, g, pt: (layer, pt[b, g * PAGES_PER_STEP + p], 0, 0))

    return pl.pallas_call(
        _sample_scores_kernel,
        grid_spec=pltpu.PrefetchScalarGridSpec(
            num_scalar_prefetch=1,
            grid=(bsz, ng),
            in_specs=[per_b(IDX_HEADS, IDX_DIM, 1), per_b(IDX_HEADS, 1, 1), per_b(IDX_DIM, 1)]
            + [page_spec(p) for p in range(PAGES_PER_STEP)],
            out_specs=[per_b(n_pages, page), per_b(8, LANE)],
            scratch_shapes=[pltpu.VMEM((IDX_HEADS, IDX_DIM, page), F32)]),
        out_shape=[jax.ShapeDtypeStruct((bsz, n_pages, page), F32),
                   jax.ShapeDtypeStruct((bsz, 8, LANE), F32)],
        compiler_params=_cparams(("arbitrary", "arbitrary")),
        name="sample_scores",
    )(page_table, qi[..., None], w[..., None, None], k_new[..., None],
      *([cache_idx_t] * PAGES_PER_STEP))


def _sample_pick_kernel(sc_ref, snew_ref, take_ref, newsel_ref, *, topk):
    n_pages, bsz, page = sc_ref.shape
    keys = _sortable(sc_ref[...])
    key_new = _sortable(snew_ref[...])

    def count_ge(cand):
        per_slot = jnp.sum(jnp.where(keys >= cand[None], 1.0, 0.0), axis=0)
        return jnp.sum(per_slot, axis=1, keepdims=True) + jnp.where(key_new >= cand, 1.0, 0.0)

    def bit_body(bit, thr):
        cand = thr ^ jnp.left_shift(jnp.int32(1), 31 - bit)
        return jnp.where(count_ge(cand) >= topk, cand, thr)

    thr = lax.fori_loop(0, 32, bit_body, jnp.full((bsz, 1), INT_MIN, I32))
    need = topk - count_ge(thr + 1)
    tri = jnp.where(lax.broadcasted_iota(I32, (page, page), 0)
                    <= lax.broadcasted_iota(I32, (page, page), 1), 1.0, 0.0).astype(BF16)
    eq = jnp.where(keys == thr[None], 1.0, 0.0)
    w_eq = jnp.dot(eq.reshape(n_pages * bsz, page).astype(BF16), tri,
                   preferred_element_type=F32).reshape(n_pages, bsz, page)
    seen = jnp.zeros((bsz, 1), F32)
    n_taken = jnp.zeros((bsz, page), F32)
    for t in range(n_pages):
        take_eq = jnp.where(w_eq[t] + seen <= need, eq[t], 0.0)
        take = jnp.where(keys[t] > thr, 1.0, take_eq)
        take_ref[t] = take
        n_taken = n_taken + take
        seen = seen + w_eq[t][:, page - 1:page]
    n_cached = jnp.sum(n_taken, axis=1, keepdims=True)
    newsel_ref[...] = jnp.broadcast_to(topk - n_cached, newsel_ref.shape)


def sample_pick(scores, s_new, *, topk):
    n_pages, bsz, page = scores.shape
    full = lambda shape: pl.BlockSpec(shape, lambda i: (0,) * len(shape))
    return pl.pallas_call(
        functools.partial(_sample_pick_kernel, topk=topk),
        grid=(1,),
        in_specs=[full(scores.shape), full(s_new.shape)],
        out_specs=[full(scores.shape), full((bsz, LANE))],
        out_shape=[jax.ShapeDtypeStruct(scores.shape, F32), jax.ShapeDtypeStruct((bsz, LANE), F32)],
        compiler_params=_cparams(("arbitrary",)),
        name="sample_pick",
    )(scores, s_new)


def _sample_attn_kernel(pt_ref, qc_ref, knc_ref, vnc_ref, take_ref, newsel_ref, near_ref, newb_ref,
                        *rest):
    npg = KV_PAGES_PER_STEP
    k_refs, v_refs = rest[:npg], rest[npg:2 * npg]
    o_ref, qb_scr, m_scr, l_scr, acc_scr = rest[2 * npg:]
    g = pl.program_id(1)
    ng = pl.num_programs(1)
    n_pages = take_ref.shape[0]

    @pl.when(g == 0)
    def _():
        qb_scr[...] = jnp.broadcast_to(qc_ref[...] * (HEAD_DIM ** -0.5), qb_scr.shape)
        m_scr[...] = jnp.full(m_scr.shape, -1e30, F32)
        l_scr[...] = jnp.zeros(l_scr.shape, F32)
        acc_scr[...] = jnp.zeros(acc_scr.shape, F32)

    logits = []
    for p in range(npg):
        t = g * npg + p
        s = jnp.sum(qb_scr[...] * k_refs[p][...], axis=1, keepdims=True)
        on_last = jnp.where(t == n_pages - 1, 1.0, 0.0)
        s = s + on_last * near_ref[...]
        logits.append(jnp.where(take_ref[pl.ds(t, 1), :][None] > 0.5, s, -jnp.inf))
    m_old = m_scr[...]
    tile_max = functools.reduce(jnp.maximum, logits)
    m_new = jnp.maximum(m_old, jnp.max(tile_max, axis=2, keepdims=True))
    alpha = jnp.exp(m_old - m_new)
    acc = acc_scr[...] * alpha
    psum = jnp.zeros_like(tile_max)
    for p in range(npg):
        pr = jnp.exp(logits[p] - m_new)
        psum = psum + pr
        acc = acc + v_refs[p][...] * pr
    acc_scr[...] = acc
    l_new = alpha * l_scr[...] + jnp.sum(psum, axis=2, keepdims=True)
    l_scr[...] = l_new
    m_scr[...] = m_new

    @pl.when(g == ng - 1)
    def _():
        q1 = qb_scr[...][:, :, :1]
        s_new = jnp.sum(q1 * knc_ref[...], axis=1, keepdims=True) + newb_ref[...][:, :, :1]
        s_new = jnp.where(newsel_ref[...][:, :1][None] > 0.5, s_new, -jnp.inf)
        m_cache = m_new[:, :, :1]
        m_fin = jnp.maximum(m_cache, s_new)
        a = jnp.exp(m_cache - m_fin)
        pn = jnp.exp(s_new - m_fin)
        ctx = jnp.sum(acc, axis=2, keepdims=True)
        o_ref[...] = (a * ctx + pn * vnc_ref[...]) / (a * l_new[:, :, :1] + pn)


def sample_attention(q, k_new, v_new, take, newsel, rel_bias, cache_k_t, cache_v_t, page_table,
                     *, layer):
    bsz, nh, hd = q.shape
    n_pages = page_table.shape[1]
    page = cache_k_t.shape[-1]
    npg = KV_PAGES_PER_STEP
    n_buckets = rel_bias.shape[0]
    assert page == LANE and n_pages % npg == 0 and page + 1 >= MAX_DISTANCE
    dist = jnp.concatenate([page - jnp.arange(page, dtype=I32), jnp.zeros((1,), I32)])
    onehot = (_t5_bucket(dist, n_buckets)[None] == jnp.arange(n_buckets)[:, None]).astype(F32)
    rel = jnp.einsum("bh,bs->hs", rel_bias - rel_bias[n_buckets - 1], onehot, precision=HI)
    near = rel[:, None, :page]
    newb = jnp.broadcast_to(rel[:, None, page:], (nh, 1, LANE))
    per_b = lambda *shape: pl.BlockSpec((None,) + shape, lambda b, g, pt: (b,) + (0,) * len(shape))
    const = lambda *shape: pl.BlockSpec(shape, lambda b, g, pt: (0,) * len(shape))

    def page_spec(p):
        return pl.BlockSpec((None, None, nh, hd, page),
                            lambda b, g, pt: (layer, pt[b, g * npg + p], 0, 0, 0))

    return pl.pallas_call(
        _sample_attn_kernel,
        grid_spec=pltpu.PrefetchScalarGridSpec(
            num_scalar_prefetch=1,
            grid=(bsz, n_pages // npg),
            in_specs=[per_b(nh, hd, 1), per_b(nh, hd, 1), per_b(nh, hd, 1), per_b(n_pages, page),
                      per_b(1, LANE), const(nh, 1, page), const(nh, 1, LANE)]
            + [page_spec(p) for p in range(npg)] * 2,
            out_specs=per_b(nh, hd, 1),
            scratch_shapes=[pltpu.VMEM((nh, hd, page), F32),
                            pltpu.VMEM((nh, 1, LANE), F32),
                            pltpu.VMEM((nh, 1, LANE), F32),
                            pltpu.VMEM((nh, hd, page), F32)]),
        out_shape=jax.ShapeDtypeStruct((bsz, nh, hd, 1), F32),
        compiler_params=_cparams(("arbitrary", "arbitrary")),
        name="sample_attn",
    )(page_table, q[..., None], k_new[..., None], v_new[..., None], take, newsel[:, None, :],
      near, newb, *([cache_k_t] * npg), *([cache_v_t] * npg))


def _sample_ssm_kernel(xbc_ref, cst_ref, z_ref, dtr_ref, h0_ref, convw_ref, convb_ref, dtb_ref, alog_ref,
                       dskip_ref, norm_ref, o_ref, h_ref, *, inner):
    N = SSM_STATE
    gw = inner // SSM_GROUPS
    conv = convb_ref[...] + convw_ref[CONV_WIDTH - 1:CONV_WIDTH, :] * xbc_ref[...]
    for jx in range(CONV_WIDTH - 1):
        conv = conv + convw_ref[jx:jx + 1, :] * cst_ref[jx:jx + 1, :]
    act = _silu(conv)
    xs = act[:, :inner]
    dt = _softplus(dtr_ref[...] + dtb_ref[...])
    decay = jnp.exp(dt * (-jnp.exp(alog_ref[...])))
    dtx = dt * xs

    def column_bcast(rowvec):
        return jnp.broadcast_to(rowvec, (LANE, inner)).T

    dec_c = column_bcast(decay)
    dtx_c = column_bcast(dtx)
    ys = []
    for g in range(SSM_GROUPS):
        rs = slice(g * gw, (g + 1) * gw)
        bg = act[:, inner + g * N:inner + (g + 1) * N]
        cg = act[:, inner + (SSM_GROUPS + g) * N:inner + (SSM_GROUPS + g + 1) * N]
        hn = h0_ref[rs, :] * dec_c[rs, :] + dtx_c[rs, :] * bg
        h_ref[rs, :] = hn
        yc = jnp.sum(hn * cg, axis=1, keepdims=True)
        ys.append(jnp.broadcast_to(yc, (gw, LANE)).T[0:1, :])
    y = jnp.concatenate(ys, axis=1)
    o_ref[...] = _gated_group_norm(y, xs, z_ref[...], dskip_ref[...], norm_ref[...])


def sample_ssm(xbc, state_conv, z, dtr_x, state, conv_w, conv_b, dt_bias, a_log, d_skip, ssm_norm):
    bsz, _, conv_dim = xbc.shape
    inner = z.shape[-1]
    rep = lambda v: jnp.repeat(v, SSM_HEAD_DIM).reshape(1, inner)
    per_b = lambda shape: pl.BlockSpec((None,) + shape, lambda b: (b, 0, 0))
    kernel = functools.partial(_sample_ssm_kernel, inner=inner)
    return pl.pallas_call(
        kernel,
        grid=(bsz,),
        in_specs=[per_b((1, conv_dim)), per_b((CONV_WIDTH - 1, conv_dim)), per_b((1, inner)),
                  per_b((1, inner)), per_b((inner, SSM_STATE)),
                  _const_spec((CONV_WIDTH, conv_dim)), _const_spec((1, conv_dim)),
                  _const_spec((1, inner)), _const_spec((1, inner)), _const_spec((1, inner)),
                  _const_spec((1, inner))],
        out_specs=[per_b((1, inner)), per_b((inner, SSM_STATE))],
        out_shape=[jax.ShapeDtypeStruct((bsz, 1, inner), F32),
                   jax.ShapeDtypeStruct((bsz, inner, SSM_STATE), F32)],
        compiler_params=_cparams(("parallel",)),
        name="sample_ssm",
    )(xbc, state_conv, z, dtr_x, state, conv_w, conv_b.reshape(1, conv_dim), rep(dt_bias),
      rep(a_log), rep(d_skip), ssm_norm.reshape(1, inner))


def _split_w_in(w_in, d_model, ssm_inner, conv_dim, ssm_heads):
    att = N_HEADS * HEAD_DIM
    sizes = (att, att, att, IDX_HEADS * IDX_DIM, IDX_DIM, IDX_HEADS, ssm_inner, conv_dim, ssm_heads,
             d_model, d_model)
    pts = [0] + [int(v) for v in np.cumsum(sizes)]
    seg = lambda a, b: w_in[:, pts[a]:pts[b]]
    w_qkv = seg(0, 3).astype(BF16)
    zeros = lambda n: jnp.zeros((w_in.shape[0], n), w_in.dtype)
    w_idx = jnp.concatenate([seg(3, 4), seg(4, 6), zeros(LANE - IDX_DIM - IDX_HEADS),
                             seg(8, 9), zeros(LANE - ssm_heads)], axis=1)
    w_z = seg(6, 7).astype(BF16)
    w_xbc = seg(7, 8).astype(BF16)
    w_gate = seg(9, 11).astype(BF16)
    w_dtx = jnp.repeat(seg(8, 9), SSM_HEAD_DIM, axis=1)
    return w_qkv, w_idx, w_z, w_xbc, w_gate, w_dtx


def kernel(x_prompt, x_sample, cache_k, cache_v, cache_idx_k, state_ssm, state_conv, page_table,
           meta_tokens, rel_bias, norm_mix, w_in, conv_w, conv_b, dt_bias, a_log, d_skip, ssm_norm,
           w_attn_out, w_ssm_out, w_out, norm_ffn, w_gate, w_up, w_down, norm_final):
    assert w_in.shape[0] == 1 and x_prompt.shape[0] == 1 and x_sample.shape[1] == 1
    d_model = x_prompt.shape[-1]
    n_meta = meta_tokens.shape[0]
    t_real = n_meta + x_prompt.shape[1]
    tp = -(-t_real // ROW_PAD) * ROW_PAD if t_real > ROW_PAD else -(-t_real // Q_TILE) * Q_TILE
    bsz = x_sample.shape[0]
    n_pool, page = cache_k.shape[1:3]
    past = page_table.shape[1] * page
    inner = ssm_norm.shape[-1]
    conv_dim = conv_w.shape[-1]
    n_ssm_heads = a_log.shape[-1]
    att = N_HEADS * HEAD_DIM
    i0 = IDX_HEADS * IDX_DIM
    l = 0
    w_qkv, w_idx, w_z, w_xbc, w_g, w_dtx = _split_w_in(w_in[l], d_model, inner, conv_dim, n_ssm_heads)
    ssm_params = (conv_w[l], conv_b[l], dt_bias[l], a_log[l], d_skip[l], ssm_norm[l])
    out_params = (w_attn_out[l], w_ssm_out[l], w_out[l], norm_ffn[l], w_gate[l], w_up[l], w_down[l],
                  norm_final)

    def project(x, tm, tag):
        proj = functools.partial(rms_matmul, x, norm_mix[l], tm=tm)
        return (proj(w_idx, tn=w_idx.shape[1], exact=True, name=tag + "_idx"),
                proj(w_z, tn=1024, name=tag + "_z"),
                proj(w_xbc, tn=1024, name=tag + "_xbc"),
                proj(w_g, tn=1024, name=tag + "_gates"))

    tm_p = 640 if tp % 640 == 0 else Q_TILE
    xp = jnp.concatenate([meta_tokens.astype(x_prompt.dtype), x_prompt[0],
                          jnp.zeros((tp - t_real, d_model), x_prompt.dtype)], axis=0)
    qkv, kv_t = rms_qkv(xp, norm_mix[l], w_qkv, tm=tm_p, name="prompt_qkv")
    idx, z, xbc, gates = project(xp, tm_p, "prompt")
    o_att = prompt_attention(qkv, idx, idx[:, i0:i0 + IDX_DIM], kv_t, rel_bias, t_real=t_real)
    o_ssm, h_t = ssd_prompt(xbc, z, idx, *ssm_params, t_real=t_real)
    y = merge_ffn(xp, o_att, o_ssm, gates, *out_params, tm=Q_TILE, name="prompt_merge_ffn")
    y_prompt = y[n_meta:t_real][None]
    k_prompt = qkv[:t_real, att:2 * att].reshape(1, 1, t_real, N_HEADS, HEAD_DIM)
    v_prompt = qkv[:t_real, 2 * att:].reshape(1, 1, t_real, N_HEADS, HEAD_DIM)
    idxk_prompt = idx[:t_real, i0:i0 + IDX_DIM][None, None]
    ssm_prompt = jnp.swapaxes(h_t, 1, 2)[None, None]
    conv_prompt = xbc[t_real - (CONV_WIDTH - 1):t_real][None, None]

    xs = x_sample[:, 0]
    qkv_s = rms_matmul(xs, norm_mix[l], w_qkv, tm=bsz, tn=1024, name="sample_qkv")
    idx_s, z_s, xbc_s, gates_s = project(xs, bsz, "sample")
    dtx_s = rms_matmul(xs, norm_mix[l], w_dtx, tm=bsz, tn=512, exact=True, name="sample_dt")
    topk_s = min(TOPK_MAX, (past + 1) // 4)
    cache_idx_t = jnp.swapaxes(cache_idx_k, 2, 3)
    cache_k_t = jnp.transpose(cache_k, (0, 1, 3, 4, 2))
    cache_v_t = jnp.transpose(cache_v, (0, 1, 3, 4, 2))
    scores, s_new = sample_scores(idx_s[:, :i0].reshape(bsz, IDX_HEADS, IDX_DIM),
                                  idx_s[:, i0 + IDX_DIM:i0 + IDX_DIM + IDX_HEADS],
                                  idx_s[:, i0:i0 + IDX_DIM], cache_idx_t, page_table, layer=l)
    take, newsel = sample_pick(jnp.swapaxes(scores, 0, 1), s_new[:, 0, :1], topk=topk_s)
    k_new = qkv_s[:, att:2 * att]
    v_new = qkv_s[:, 2 * att:]
    heads = lambda a: a.reshape(bsz, N_HEADS, HEAD_DIM)
    o_att_s = sample_attention(heads(qkv_s[:, :att]), heads(k_new), heads(v_new),
                               jnp.swapaxes(take, 0, 1), newsel, rel_bias, cache_k_t, cache_v_t,
                               page_table, layer=l)
    o_ssm_s, state_new = sample_ssm(xbc_s[:, None], state_conv[l], z_s[:, None], dtx_s[:, None],
                                    state_ssm[l].reshape(bsz, inner, SSM_STATE), *ssm_params)
    y_s = merge_ffn(xs, o_att_s.reshape(bsz, att), o_ssm_s.reshape(bsz, inner), gates_s, *out_params,
                    tm=bsz, name="sample_merge_ffn")
    y_sample = y_s[:, None]
    k_sample = k_new.reshape(1, bsz, 1, N_HEADS, HEAD_DIM)
    v_sample = v_new.reshape(1, bsz, 1, N_HEADS, HEAD_DIM)
    idxk_sample = idx_s[:, i0:i0 + IDX_DIM].reshape(1, bsz, 1, IDX_DIM)
    ssm_sample = state_new.reshape((1,) + state_ssm.shape[1:])
    conv_sample = jnp.concatenate([state_conv[l][:, 1:], xbc_s[:, None]], axis=1)[None]
    return (y_prompt, y_sample, k_prompt, v_prompt, idxk_prompt, ssm_prompt, conv_prompt,
            k_sample, v_sample, idxk_sample, ssm_sample, conv_sample)
```
